```python
import math
import jax, jax.numpy as jnp
from jax import lax
import numpy as np

D_MODEL = 1024
BATCH = 32
SEQ = 2048
DEPTH = 2

N_MIXERS = 2
N_META = 16
N_HEADS = 8
HEAD_DIM = 64
V_DIM = 2 * HEAD_DIM
QKV_DIM = 3 * N_HEADS * 2 * HEAD_DIM
ROPE_THETA = 10000.0
CONV_WIDTH = 3
D_FF = -(-8 * D_MODEL // (3 * 256)) * 256
Q_BLOCK = 128
EPS = 1e-6
N_ATTN_LAYERS = (DEPTH + 1) // 2
N_CONV_LAYERS = DEPTH // 2

kernel_name = "hybrid_diffattn_shortconv_meta"


def rms_norm(x, g):
    xf = x.astype(jnp.float32)
    y = xf * lax.rsqrt(jnp.mean(xf * xf, axis=-1, keepdims=True) + EPS)
    return (y * g.astype(jnp.float32)).astype(x.dtype)


def rope_tables(length):
    inv = 1.0 / (ROPE_THETA ** (jnp.arange(0, HEAD_DIM, 2, dtype=jnp.float32) / HEAD_DIM))
    pos = jnp.arange(length, dtype=jnp.float32)
    ang = pos[:, None] * inv[None, :]
    return jnp.cos(ang), jnp.sin(ang)


def apply_rope(x, cos, sin):
    xf = x.astype(jnp.float32)
    x1, x2 = jnp.split(xf, 2, axis=-1)
    out = jnp.concatenate([x1 * cos - x2 * sin, x2 * cos + x1 * sin], axis=-1)
    return out.astype(x.dtype)


def diff_attention(h, w_qkv, q_gain, k_gain, lam_q1, lam_k1, lam_q2, lam_k2,
                   sub_gain, w_o, lambda_init):
    b, length, _ = h.shape
    qkv = h @ w_qkv
    q, k, v = jnp.split(qkv, 3, axis=-1)
    q = q.reshape(b, length, N_HEADS, 2, HEAD_DIM).transpose(0, 2, 3, 1, 4)
    k = k.reshape(b, length, N_HEADS, 2, HEAD_DIM).transpose(0, 2, 3, 1, 4)
    v = v.reshape(b, length, N_HEADS, V_DIM).transpose(0, 2, 1, 3)
    cos, sin = rope_tables(length)
    q = apply_rope(rms_norm(q, q_gain), cos, sin) * (HEAD_DIM ** -0.5)
    k = apply_rope(rms_norm(k, k_gain), cos, sin)
    lam = (jnp.exp(jnp.sum(lam_q1.astype(jnp.float32) * lam_k1.astype(jnp.float32)))
           - jnp.exp(jnp.sum(lam_q2.astype(jnp.float32) * lam_k2.astype(jnp.float32)))
           + lambda_init)
    bounds = [(0, N_META)] + [(s, s + Q_BLOCK) for s in range(N_META, length, Q_BLOCK)]
    outs = []
    for qs, qe in bounds:
        qb = q[:, :, :, qs:qe]
        kb = k[:, :, :, :qe]
        vb = v[:, :, :qe]
        s = jnp.einsum('bhcqd,bhckd->bhcqk', qb, kb).astype(jnp.float32)
        mask = jnp.arange(qs, qe)[:, None] >= jnp.arange(qe)[None, :]
        s = jnp.where(mask, s, -jnp.inf)
        p = jax.nn.softmax(s, axis=-1)
        a = p[:, :, 0] - lam * p[:, :, 1]
        outs.append(jnp.einsum('bhqk,bhkd->bhqd', a.astype(vb.dtype), vb))
    o = jnp.concatenate(outs, axis=2)
    o = rms_norm(o, sub_gain) * (1.0 - lambda_init)
    o = o.transpose(0, 2, 1, 3).reshape(b, length, N_HEADS * V_DIM)
    return o @ w_o


def short_conv(h, w_in, conv_w, w_out):
    bcu = h @ w_in
    gate_b, gate_c, u = jnp.split(bcu, 3, axis=-1)
    z = gate_c * u
    zc = lax.conv_general_dilated(
        z, conv_w.astype(z.dtype), window_strides=(1,),
        padding=[(CONV_WIDTH - 1, 0)],
        dimension_numbers=('NWC', 'WIO', 'NWC'),
        feature_group_count=D_MODEL)
    return (gate_b * zc) @ w_out


def swiglu(h, w_gate_up, w_down):
    g, u = jnp.split(h @ w_gate_up, 2, axis=-1)
    return (jax.nn.silu(g) * u) @ w_down


def setup_inputs(seed: int = 0) -> dict:
    key = jax.random.key(seed)
    ks = jax.random.split(key, 20)
    f32 = jnp.float32
    nrm = lambda k, shape, scale: jax.random.normal(k, shape, f32) * scale
    gain = lambda k, shape: 1.0 + 0.02 * jax.random.normal(k, shape, f32)
    return {
        "x": nrm(ks[0], (BATCH, SEQ, D_MODEL), 1.0),
        "meta_tokens": nrm(ks[1], (N_META, D_MODEL), 1.0),
        "mixer_norm_g": gain(ks[2], (DEPTH, D_MODEL)),
        "ffn_norm_g": gain(ks[3], (DEPTH, D_MODEL)),
        "attn_w_qkv": nrm(ks[4], (N_ATTN_LAYERS, D_MODEL, QKV_DIM), D_MODEL ** -0.5),
        "attn_q_gain": gain(ks[5], (N_ATTN_LAYERS, HEAD_DIM)),
        "attn_k_gain": gain(ks[6], (N_ATTN_LAYERS, HEAD_DIM)),
        "attn_lambda_q1": nrm(ks[7], (N_ATTN_LAYERS, HEAD_DIM), 0.1),
        "attn_lambda_k1": nrm(ks[8], (N_ATTN_LAYERS, HEAD_DIM), 0.1),
        "attn_lambda_q2": nrm(ks[9], (N_ATTN_LAYERS, HEAD_DIM), 0.1),
        "attn_lambda_k2": nrm(ks[10], (N_ATTN_LAYERS, HEAD_DIM), 0.1),
        "attn_sub_gain": gain(ks[11], (N_ATTN_LAYERS, V_DIM)),
        "attn_w_o": nrm(ks[12], (N_ATTN_LAYERS, N_HEADS * V_DIM, D_MODEL), (N_HEADS * V_DIM) ** -0.5),
        "conv_w_in": nrm(ks[13], (N_CONV_LAYERS, D_MODEL, 3 * D_MODEL), D_MODEL ** -0.5),
        "conv_w": nrm(ks[14], (N_CONV_LAYERS, CONV_WIDTH, 1, D_MODEL), CONV_WIDTH ** -0.5),
        "conv_w_out": nrm(ks[15], (N_CONV_LAYERS, D_MODEL, D_MODEL), D_MODEL ** -0.5),
        "ffn_w_gate_up": nrm(ks[16], (DEPTH, D_MODEL, 2 * D_FF), D_MODEL ** -0.5),
        "ffn_w_down": nrm(ks[17], (DEPTH, D_FF, D_MODEL), D_FF ** -0.5),
    }


def reference(x, meta_tokens, mixer_norm_g, ffn_norm_g, attn_w_qkv, attn_q_gain,
              attn_k_gain, attn_lambda_q1, attn_lambda_k1, attn_lambda_q2,
              attn_lambda_k2, attn_sub_gain, attn_w_o, conv_w_in, conv_w,
              conv_w_out, ffn_w_gate_up, ffn_w_down):
    b = x.shape[0]
    meta = jnp.broadcast_to(meta_tokens.astype(x.dtype)[None], (b, N_META, D_MODEL))
    h = jnp.concatenate([meta, x], axis=1)
    for i in range(DEPTH):
        hn = rms_norm(h, mixer_norm_g[i])
        j = i // N_MIXERS
        if i % N_MIXERS == 0:
            lambda_init = 0.8 - 0.6 * math.exp(-0.3 * i)
            h = h + diff_attention(hn, attn_w_qkv[j], attn_q_gain[j], attn_k_gain[j],
                                   attn_lambda_q1[j], attn_lambda_k1[j],
                                   attn_lambda_q2[j], attn_lambda_k2[j],
                                   attn_sub_gain[j], attn_w_o[j], lambda_init)
        else:
            h = h + short_conv(hn, conv_w_in[j], conv_w[j], conv_w_out[j])
        h = h + swiglu(rms_norm(h, ffn_norm_g[i]), ffn_w_gate_up[i], ffn_w_down[i])
    return h[:, N_META:]
```

```python
import functools

import jax
import jax.numpy as jnp
from jax import lax
from jax.experimental import pallas as pl
from jax.experimental.pallas import tpu as pltpu

D_MODEL = 1024
N_META = 16
N_HEADS = 8
HEAD_DIM = 64
V_DIM = 2 * HEAD_DIM
D_FF = 2816
ROPE_THETA = 10000.0
CONV_WIDTH = 3
EPS = 1e-6
LAMBDA_INIT_L0 = 0.8 - 0.6

V7X_LANES = 128
META_PAD = V7X_LANES
V7X_VMEM_LIMIT_BYTES = 56 * 1024 * 1024
MASKED = -1e30
FF_CHUNKS = ((0, 1024), (1024, 2048), (2048, D_FF))

_F32 = jnp.float32
_BF16 = jnp.bfloat16


def _dot(a, b):
    return jnp.dot(a, b, preferred_element_type=_F32)


def _rmsnorm_rows(hT, g_col):
    ms = jnp.mean(hT * hT, axis=0, keepdims=True)
    return hT * lax.rsqrt(ms + EPS) * g_col


def _qkv_kernel(x_ref, g_ref, wT_ref, tabq_ref, tabk_ref, qT_ref, k_ref, vT_ref):
    x = x_ref[0]
    ms = jnp.mean(x * x, axis=-1, keepdims=True)
    hn = (x * lax.rsqrt(ms + EPS) * g_ref[...]).astype(_BF16)
    yT = lax.dot_general(wT_ref[...], hn, (((1,), (1,)), ((), ())),
                         preferred_element_type=_F32)
    tabq = tabq_ref[...]
    tabk = tabk_ref[...]
    half = HEAD_DIM // 2

    def norm_rope(y, tab):
        yn = y * lax.rsqrt(jnp.mean(y * y, axis=0, keepdims=True) + EPS)
        x1, x2 = yn[:half], yn[half:]
        return jnp.concatenate([x1 * tab[0] - x2 * tab[1], x2 * tab[2] + x1 * tab[3]], axis=0)

    for i in range(2 * N_HEADS):
        r = i * HEAD_DIM
        qT_ref[0, r:r + HEAD_DIM, :] = norm_rope(yT[r:r + HEAD_DIM], tabq).astype(_BF16)
    for h in range(N_HEADS):
        r = D_MODEL + h * V_DIM
        kh = jnp.concatenate([norm_rope(yT[r:r + HEAD_DIM], tabk),
                              norm_rope(yT[r + HEAD_DIM:r + V_DIM], tabk)], axis=0)
        k_ref[0, :, h * V_DIM:(h + 1) * V_DIM] = kh.T.astype(_BF16)
    vT_ref[0] = yT[2 * D_MODEL:].astype(_BF16)


def _qkv_call(x, g_row, wT, tabq, tabk, tm):
    b, s, d = x.shape
    grid = (b, s // tm)
    const = lambda shape: pl.BlockSpec(shape, lambda i, t: (0,) * len(shape),
                                       pipeline_mode=pl.Buffered(1))
    return pl.pallas_call(
        _qkv_kernel,
        grid=grid,
        in_specs=[
            pl.BlockSpec((1, tm, d), lambda i, t: (i, t, 0)),
            const((1, d)),
            const((3 * d, d)),
            pl.BlockSpec((4, HEAD_DIM // 2, tm), lambda i, t: (0, 0, t)),
            pl.BlockSpec((4, HEAD_DIM // 2, tm), lambda i, t: (0, 0, t)),
        ],
        out_specs=[
            pl.BlockSpec((1, d, tm), lambda i, t: (i, 0, t)),
            pl.BlockSpec((1, tm, d), lambda i, t: (i, t, 0)),
            pl.BlockSpec((1, d, tm), lambda i, t: (i, 0, t)),
        ],
        out_shape=[
            jax.ShapeDtypeStruct((b, d, s), _BF16),
            jax.ShapeDtypeStruct((b, s, d), _BF16),
            jax.ShapeDtypeStruct((b, d, s), _BF16),
        ],
        compiler_params=pltpu.CompilerParams(
            dimension_semantics=("arbitrary", "arbitrary"),
            vmem_limit_bytes=V7X_VMEM_LIMIT_BYTES),
        name="qkv_proj",
    )(x, g_row, wT, tabq, tabk)


def _flash_block(kb, vTb, q_maps, mask, state):
    new_state = []
    for q_pad, (m, l, acc) in zip(q_maps, state):
        s = _dot(kb, q_pad)
        if mask is not None:
            s = jnp.where(mask, s, MASKED)
        m_new = jnp.maximum(m, jnp.max(s, axis=0, keepdims=True))
        p = jnp.exp(s - m_new)
        alpha = jnp.exp(m - m_new)
        l = alpha * l + jnp.sum(p, axis=0, keepdims=True)
        acc = alpha * acc + _dot(vTb, p.astype(_BF16))
        new_state.append((m_new, l, acc))
    return new_state


def _attn_kernel(qT_ref, k_ref, vT_ref, km_ref, vmT_ref, lq1_ref, lk1_ref, lq2_ref, lk2_ref,
                 gsub_ref, oT_ref, *, n_qblk, tq, pos_off, has_real):
    lam = (jnp.exp(jnp.sum(lq1_ref[...] * lk1_ref[...], keepdims=True))
           - jnp.exp(jnp.sum(lq2_ref[...] * lk2_ref[...], keepdims=True))
           + LAMBDA_INIT_L0)
    km = km_ref[...]
    vmT = vmT_ref[...]
    zero_half = jnp.zeros((HEAD_DIM, tq), _BF16)
    key_row = lax.broadcasted_iota(jnp.int32, (tq, tq), 0)
    qry_col = lax.broadcasted_iota(jnp.int32, (tq, tq), 1)
    diag_mask = key_row <= qry_col
    meta_row = lax.broadcasted_iota(jnp.int32, (META_PAD, tq), 0)
    meta_col = lax.broadcasted_iota(jnp.int32, (META_PAD, tq), 1)

    for qi in range(n_qblk):
        qT = qT_ref[0, :, qi * tq:(qi + 1) * tq]
        q_maps = (jnp.concatenate([qT[:HEAD_DIM], zero_half], axis=0),
                  jnp.concatenate([zero_half, qT[HEAD_DIM:]], axis=0))
        init = (jnp.full((1, tq), MASKED, _F32), jnp.zeros((1, tq), _F32),
                jnp.zeros((V_DIM, tq), _F32))
        state = [init, init]
        meta_mask = (meta_row < N_META) & (meta_row <= meta_col + (pos_off + qi * tq))
        state = _flash_block(km, vmT, q_maps, meta_mask, state)
        if has_real:
            for j in range(qi):
                state = _flash_block(k_ref[0, j * tq:(j + 1) * tq, :],
                                     vT_ref[0, :, j * tq:(j + 1) * tq], q_maps, None, state)
            state = _flash_block(k_ref[0, qi * tq:(qi + 1) * tq, :],
                                 vT_ref[0, :, qi * tq:(qi + 1) * tq], q_maps, diag_mask, state)
        (_, l1, a1), (_, l2, a2) = state
        o = a1 * (1.0 / l1) - a2 * (lam / l2)
        on = _rmsnorm_rows(o, gsub_ref[...]) * (1.0 - LAMBDA_INIT_L0)
        oT_ref[0, :, qi * tq:(qi + 1) * tq] = on.astype(_BF16)


def _attn_call(qT, k, vT, k_meta, vT_meta, lam_params, gsub_col, *, tq, pos_off, has_real):
    b, d, s = qT.shape
    kernel = functools.partial(_attn_kernel, n_qblk=s // tq, tq=tq, pos_off=pos_off,
                               has_real=has_real)
    lam_spec = pl.BlockSpec((1, HEAD_DIM), lambda i, h: (0, 0))
    return pl.pallas_call(
        kernel,
        grid=(b, N_HEADS),
        in_specs=[
            pl.BlockSpec((1, V_DIM, s), lambda i, h: (i, h, 0)),
            pl.BlockSpec((1, s, V_DIM), lambda i, h: (i, 0, h)),
            pl.BlockSpec((1, V_DIM, s), lambda i, h: (i, h, 0)),
            pl.BlockSpec((META_PAD, V_DIM), lambda i, h: (0, h)),
            pl.BlockSpec((V_DIM, META_PAD), lambda i, h: (h, 0)),
            lam_spec, lam_spec, lam_spec, lam_spec,
            pl.BlockSpec((V_DIM, 1), lambda i, h: (0, 0)),
        ],
        out_specs=pl.BlockSpec((1, V_DIM, s), lambda i, h: (i, h, 0)),
        out_shape=jax.ShapeDtypeStruct((b, d, s), _BF16),
        compiler_params=pltpu.CompilerParams(
            dimension_semantics=("arbitrary", "arbitrary"),
            vmem_limit_bytes=V7X_VMEM_LIMIT_BYTES),
        name="diff_attn",
    )(qT, k, vT, k_meta, vT_meta, *lam_params, gsub_col)


def _ffn_T(hT, g_col, wguT_ref, wdT_ref):
    hn = _rmsnorm_rows(hT, g_col).astype(_BF16)
    acc = None
    for c0, c1 in FF_CHUNKS:
        g = _dot(wguT_ref[c0:c1, :], hn)
        u = _dot(wguT_ref[D_FF + c0:D_FF + c1, :], hn)
        a = (g * jax.nn.sigmoid(g) * u).astype(_BF16)
        part = _dot(wdT_ref[:, c0:c1], a)
        acc = part if acc is None else acc + part
    return acc


def _attn_out_ffn_kernel(x_ref, oT_ref, woT_ref, g_ref, wguT_ref, wdT_ref, hT_ref):
    h = x_ref[0].T + _dot(woT_ref[...], oT_ref[0])
    hT_ref[0] = h + _ffn_T(h, g_ref[...], wguT_ref, wdT_ref)


def _attn_out_ffn_call(x, oT, woT, g_col, wguT, wdT, tm):
    b, s, d = x.shape
    const = lambda shape: pl.BlockSpec(shape, lambda i, t: (0,) * len(shape),
                                       pipeline_mode=pl.Buffered(1))
    return pl.pallas_call(
        _attn_out_ffn_kernel,
        grid=(b, s // tm),
        in_specs=[
            pl.BlockSpec((1, tm, d), lambda i, t: (i, t, 0)),
            pl.BlockSpec((1, d, tm), lambda i, t: (i, 0, t)),
            const((d, d)), const((d, 1)), const((2 * D_FF, d)), const((d, D_FF)),
        ],
        out_specs=pl.BlockSpec((1, d, tm), lambda i, t: (i, 0, t)),
        out_shape=jax.ShapeDtypeStruct((b, d, s), _F32),
        compiler_params=pltpu.CompilerParams(
            dimension_semantics=("arbitrary", "arbitrary"),
            vmem_limit_bytes=V7X_VMEM_LIMIT_BYTES),
        name="attn_out_ffn",
    )(x, oT, woT, g_col, wguT, wdT)


def _conv_gate_T(hT, g_col, winT_ref):
    hn = _rmsnorm_rows(hT, g_col).astype(_BF16)
    bcu = _dot(winT_ref[...], hn)
    return bcu[:D_MODEL], bcu[D_MODEL:2 * D_MODEL] * bcu[2 * D_MODEL:]


def _conv_z_kernel(hT_ref, g_ref, winT_ref, zT_ref):
    _, z = _conv_gate_T(hT_ref[0], g_ref[...], winT_ref)
    zT_ref[0] = z


def _conv_ffn_kernel(hT_ref, halo_ref, g1_ref, winT_ref, cw_ref, woutT_ref, g2_ref, wguT_ref,
                     wdT_ref, out_ref, zprev_ref, *, tm):
    @pl.when(pl.program_id(1) == 0)
    def _():
        zprev_ref[...] = halo_ref[...]

    h = hT_ref[0]
    gate_b, z = _conv_gate_T(h, g1_ref[...], winT_ref)
    zext = jnp.concatenate([zprev_ref[...], z], axis=1)
    z1 = pltpu.roll(zext, 1, 1)[:, V7X_LANES:]
    z2 = pltpu.roll(zext, 2, 1)[:, V7X_LANES:]
    zc = cw_ref[2] * z + cw_ref[1] * z1 + cw_ref[0] * z2
    zprev_ref[...] = z[:, tm - V7X_LANES:]
    h = h + _dot(woutT_ref[...], (gate_b * zc).astype(_BF16))
    h = h + _ffn_T(h, g2_ref[...], wguT_ref, wdT_ref)
    out_ref[0] = h.T


def _conv_z_call(hT, g_col, winT):
    b, d, s = hT.shape
    return pl.pallas_call(
        _conv_z_kernel,
        grid=(b,),
        in_specs=[
            pl.BlockSpec((1, d, s), lambda i: (i, 0, 0)),
            pl.BlockSpec((d, 1), lambda i: (0, 0)),
            pl.BlockSpec((3 * d, d), lambda i: (0, 0)),
        ],
        out_specs=pl.BlockSpec((1, d, s), lambda i: (i, 0, 0)),
        out_shape=jax.ShapeDtypeStruct((b, d, s), _F32),
        compiler_params=pltpu.CompilerParams(
            dimension_semantics=("arbitrary",),
            vmem_limit_bytes=V7X_VMEM_LIMIT_BYTES),
        name="conv_z_meta",
    )(hT, g_col, winT)


def _conv_ffn_call(hT, halo, g1_col, winT, cw, woutT, g2_col, wguT, wdT, tm):
    b, d, s = hT.shape
    const = lambda shape: pl.BlockSpec(shape, lambda i, t: (0,) * len(shape),
                                       pipeline_mode=pl.Buffered(1))
    return pl.pallas_call(
        functools.partial(_conv_ffn_kernel, tm=tm),
        grid=(b, s // tm),
        in_specs=[
            pl.BlockSpec((1, d, tm), lambda i, t: (i, 0, t)),
            const((d, V7X_LANES)), const((d, 1)), const((3 * d, d)),
            const((CONV_WIDTH, d, 1)), const((d, d)), const((d, 1)),
            const((2 * D_FF, d)), const((d, D_FF)),
        ],
        out_specs=pl.BlockSpec((1, tm, d), lambda i, t: (i, t, 0)),
        out_shape=jax.ShapeDtypeStruct((b, s, d), _F32),
        scratch_shapes=[pltpu.VMEM((d, V7X_LANES), _F32)],
        compiler_params=pltpu.CompilerParams(
            dimension_semantics=("arbitrary", "arbitrary"),
            vmem_limit_bytes=V7X_VMEM_LIMIT_BYTES),
        name="conv_ffn",
    )(hT, halo, g1_col, winT, cw, woutT, g2_col, wguT, wdT)


def _rope_tables(gain, scale, pos0, n):
    half = HEAD_DIM // 2
    inv = 1.0 / (ROPE_THETA ** (jnp.arange(0, HEAD_DIM, 2, dtype=_F32) / HEAD_DIM))
    pos = jnp.arange(pos0, pos0 + n, dtype=_F32)
    ang = pos[None, :] * inv[:, None]
    cos, sin = jnp.cos(ang), jnp.sin(ang)
    g1 = gain[:half, None].astype(_F32) * scale
    g2 = gain[half:, None].astype(_F32) * scale
    return jnp.stack([g1 * cos, g2 * sin, g2 * cos, g1 * sin])


def _token_tile(s):
    for tm in (512, 256, 128):
        if s % tm == 0:
            return tm
    raise ValueError(f"sequence length {s} must be a multiple of 128")


def kernel(x, meta_tokens, mixer_norm_g, ffn_norm_g, attn_w_qkv, attn_q_gain, attn_k_gain, attn_lambda_q1, attn_lambda_k1, attn_lambda_q2, attn_lambda_k2, attn_sub_gain, attn_w_o, conv_w_in, conv_w, conv_w_out, ffn_w_gate_up, ffn_w_down):
    b, s, d = x.shape
    assert d == D_MODEL and meta_tokens.shape == (N_META, D_MODEL)
    assert mixer_norm_g.shape[0] == 2, "layer 0 = differential attention, layer 1 = short conv"
    tm = _token_tile(s)
    tq = 256 if s % 256 == 0 else 128

    wqkvT = attn_w_qkv[0].T.astype(_BF16)
    woT = attn_w_o[0].T.astype(_BF16)
    winT = conv_w_in[0].T.astype(_BF16)
    woutT = conv_w_out[0].T.astype(_BF16)
    wguT = [ffn_w_gate_up[i].T.astype(_BF16) for i in range(2)]
    wdT = [ffn_w_down[i].T.astype(_BF16) for i in range(2)]
    cw = conv_w[0].reshape(CONV_WIDTH, D_MODEL, 1).astype(_F32)
    g_mix0_row = mixer_norm_g[0].reshape(1, D_MODEL)
    g_mix1_col = mixer_norm_g[1].reshape(D_MODEL, 1)
    g_ffn_col = [ffn_norm_g[i].reshape(D_MODEL, 1) for i in range(2)]
    gsub_col = attn_sub_gain[0].reshape(V_DIM, 1)
    lam_params = [p[0].reshape(1, HEAD_DIM) for p in
                  (attn_lambda_q1, attn_lambda_k1, attn_lambda_q2, attn_lambda_k2)]
    q_scale = HEAD_DIM ** -0.5

    x_meta = jnp.zeros((1, META_PAD, D_MODEL), x.dtype).at[0, :N_META].set(meta_tokens.astype(x.dtype))
    tabq_m = _rope_tables(attn_q_gain[0], q_scale, 0, META_PAD)
    tabk_m = _rope_tables(attn_k_gain[0], 1.0, 0, META_PAD)
    qT_m, k_m, vT_m = _qkv_call(x_meta, g_mix0_row, wqkvT, tabq_m, tabk_m, META_PAD)
    oT_m = _attn_call(qT_m, k_m, vT_m, k_m[0], vT_m[0], lam_params, gsub_col,
                      tq=META_PAD, pos_off=0, has_real=False)
    h1T_m = _attn_out_ffn_call(x_meta, oT_m, woT, g_ffn_col[0], wguT[0], wdT[0], META_PAD)
    zT_m = _conv_z_call(h1T_m, g_mix1_col, winT)
    halo = jnp.zeros((D_MODEL, V7X_LANES), _F32).at[:, V7X_LANES - 2:].set(
        zT_m[0][:, N_META - 2:N_META])

    tabq = _rope_tables(attn_q_gain[0], q_scale, N_META, s)
    tabk = _rope_tables(attn_k_gain[0], 1.0, N_META, s)
    qT, k, vT = _qkv_call(x, g_mix0_row, wqkvT, tabq, tabk, tm)
    oT = _attn_call(qT, k, vT, k_m[0], vT_m[0], lam_params, gsub_col,
                    tq=tq, pos_off=N_META, has_real=True)
    h1T = _attn_out_ffn_call(x, oT, woT, g_ffn_col[0], wguT[0], wdT[0], tm)
    return _conv_ffn_call(h1T, halo, g_mix1_col, winT, cw, woutT, g_ffn_col[1], wguT[1], wdT[1], tm)
```

```python
import functools
import math

import jax
import jax.numpy as jnp
from jax import lax
from jax.experimental import pallas as pl
from jax.experimental.pallas import tpu as pltpu

D_MODEL = 1024
N_META = 16
N_HEADS = 8
HEAD_DIM = 64
V_DIM = 2 * HEAD_DIM
D_FF = 2816
ROPE_THETA = 10000.0
CONV_WIDTH = 3
EPS = 1e-6
LAMBDA_INIT_L0 = 0.8 - 0.6

V7X_LANES = 128
META_PAD = V7X_LANES
V7X_VMEM_LIMIT_BYTES = 56 * 1024 * 1024
MASKED = -1e30
FF_CHUNKS = ((0, 1024), (1024, 2048), (2048, D_FF))

_F32 = jnp.float32
_BF16 = jnp.bfloat16


def _dot(a, b):
    return jnp.dot(a, b, preferred_element_type=_F32)


def _rmsnorm_rows(hT, g_col):
    ms = jnp.mean(hT * hT, axis=0, keepdims=True)
    return hT * lax.rsqrt(ms + EPS) * g_col


def _qkv_kernel(x_ref, g_ref, wT_ref, tabq_ref, tabk_ref, qT_ref, k_ref, vT_ref):
    x = x_ref[0]
    ms = jnp.mean(x * x, axis=-1, keepdims=True)
    hn = (x * lax.rsqrt(ms + EPS) * g_ref[...]).astype(_BF16)
    def proj(sec):
        return lax.dot_general(wT_ref[sec * D_MODEL:(sec + 1) * D_MODEL, :], hn,
                               (((1,), (1,)), ((), ())), preferred_element_type=_F32)

    qf, kf, vf = proj(0), proj(1), proj(2)
    tabq = tabq_ref[...]
    tabk = tabk_ref[...]
    half = HEAD_DIM // 2

    def norm_rope(y, tab):
        yn = y * lax.rsqrt(jnp.mean(y * y, axis=0, keepdims=True) + EPS)
        x1, x2 = yn[:half], yn[half:]
        return jnp.concatenate([x1 * tab[0] - x2 * tab[1], x2 * tab[2] + x1 * tab[3]], axis=0)

    for i in range(2 * N_HEADS):
        r = i * HEAD_DIM
        qT_ref[0, r:r + HEAD_DIM, :] = norm_rope(qf[r:r + HEAD_DIM], tabq).astype(_BF16)
    for h in range(N_HEADS):
        r = h * V_DIM
        kh = jnp.concatenate([norm_rope(kf[r:r + HEAD_DIM], tabk),
                              norm_rope(kf[r + HEAD_DIM:r + V_DIM], tabk)], axis=0)
        k_ref[0, :, h * V_DIM:(h + 1) * V_DIM] = kh.T.astype(_BF16)
    vT_ref[0] = vf.astype(_BF16)


def _qkv_call(x, g_row, wT, tabq, tabk, tm):
    b, s, d = x.shape
    grid = (b, s // tm)
    const = lambda shape: pl.BlockSpec(shape, lambda i, t: (0,) * len(shape),
                                       pipeline_mode=pl.Buffered(1))
    return pl.pallas_call(
        _qkv_kernel,
        grid=grid,
        in_specs=[
            pl.BlockSpec((1, tm, d), lambda i, t: (i, t, 0)),
            const((1, d)),
            const((3 * d, d)),
            pl.BlockSpec((4, HEAD_DIM // 2, tm), lambda i, t: (0, 0, t)),
            pl.BlockSpec((4, HEAD_DIM // 2, tm), lambda i, t: (0, 0, t)),
        ],
        out_specs=[
            pl.BlockSpec((1, d, tm), lambda i, t: (i, 0, t)),
            pl.BlockSpec((1, tm, d), lambda i, t: (i, t, 0)),
            pl.BlockSpec((1, d, tm), lambda i, t: (i, 0, t)),
        ],
        out_shape=[
            jax.ShapeDtypeStruct((b, d, s), _BF16),
            jax.ShapeDtypeStruct((b, s, d), _BF16),
            jax.ShapeDtypeStruct((b, d, s), _BF16),
        ],
        compiler_params=pltpu.CompilerParams(
            dimension_semantics=("arbitrary", "arbitrary"),
            vmem_limit_bytes=V7X_VMEM_LIMIT_BYTES),
        name="qkv_proj",
    )(x, g_row, wT, tabq, tabk)


def _attn_kernel(qT_ref, k_ref, vT_ref, km_ref, vmT_ref, lq1_ref, lk1_ref, lq2_ref, lk2_ref,
                 gsub_ref, oT_ref, s_ref, *, n_qblk, tq, pos_off, has_real):
    lam = (jnp.exp(jnp.sum(lq1_ref[...] * lk1_ref[...], keepdims=True))
           - jnp.exp(jnp.sum(lq2_ref[...] * lk2_ref[...], keepdims=True))
           + LAMBDA_INIT_L0)
    zero_half = jnp.zeros((HEAD_DIM, tq), _BF16)
    key_row = lax.broadcasted_iota(jnp.int32, (tq, 2 * tq), 0)
    qry_col = lax.broadcasted_iota(jnp.int32, (tq, 2 * tq), 1) % tq
    diag_mask = key_row <= qry_col
    meta_row = lax.broadcasted_iota(jnp.int32, (META_PAD, 2 * tq), 0)
    meta_col = lax.broadcasted_iota(jnp.int32, (META_PAD, 2 * tq), 1) % tq

    def col_reduce(fn, v):
        return fn(v.reshape(v.shape[0] // 8, 8, v.shape[1]), axis=0)

    for qi in range(n_qblk):
        qT = qT_ref[0, :, qi * tq:(qi + 1) * tq]
        qab = jnp.concatenate([jnp.concatenate([qT[:HEAD_DIM], zero_half], axis=0),
                               jnp.concatenate([zero_half, qT[HEAD_DIM:]], axis=0)], axis=1)
        meta_mask = (meta_row < N_META) & (meta_row <= meta_col + (pos_off + qi * tq))
        chunks = [(0, META_PAD, km_ref[...], vmT_ref[...], meta_mask)]
        if has_real:
            for j in range(qi + 1):
                chunks.append((META_PAD + j * tq, tq, k_ref[0, j * tq:(j + 1) * tq, :],
                               vT_ref[0, :, j * tq:(j + 1) * tq], diag_mask if j == qi else None))
        s_buf = s_ref.at[qi % 2]
        m8 = jnp.full((8, 2 * tq), MASKED, _F32)
        for r0, nr, kc, _, mask in chunks:
            s = _dot(kc, qab)
            if mask is not None:
                s = jnp.where(mask, s, MASKED)
            s_buf[r0:r0 + nr, :] = s
            m8 = jnp.maximum(m8, col_reduce(jnp.max, s))
        m = jnp.max(m8, axis=0, keepdims=True)
        l8 = jnp.zeros((8, 2 * tq), _F32)
        for r0, nr, _, _, _ in chunks:
            l8 = l8 + col_reduce(jnp.sum, jnp.exp2(s_buf[r0:r0 + nr, :] - m))
        l = jnp.sum(l8, axis=0, keepdims=True)
        inv = jnp.concatenate([1.0 / l[:, :tq], lam / l[:, tq:]], axis=1)
        acc = jnp.zeros((V_DIM, tq), _F32)
        for r0, nr, _, vc, _ in chunks:
            e = jnp.exp2(s_buf[r0:r0 + nr, :] - m) * inv
            acc = acc + _dot(vc, (e[:, :tq] - e[:, tq:]).astype(_BF16))
        on = _rmsnorm_rows(acc, gsub_ref[...]) * (1.0 - LAMBDA_INIT_L0)
        oT_ref[0, :, qi * tq:(qi + 1) * tq] = on.astype(_BF16)


def _attn_call(qT, k, vT, k_meta, vT_meta, lam_params, gsub_col, *, tq, pos_off, has_real):
    b, d, s = qT.shape
    kernel = functools.partial(_attn_kernel, n_qblk=s // tq, tq=tq, pos_off=pos_off,
                               has_real=has_real)
    lam_spec = pl.BlockSpec((1, HEAD_DIM), lambda i, h: (0, 0))
    return pl.pallas_call(
        kernel,
        grid=(b, N_HEADS),
        in_specs=[
            pl.BlockSpec((1, V_DIM, s), lambda i, h: (i, h, 0)),
            pl.BlockSpec((1, s, V_DIM), lambda i, h: (i, 0, h)),
            pl.BlockSpec((1, V_DIM, s), lambda i, h: (i, h, 0)),
            pl.BlockSpec((META_PAD, V_DIM), lambda i, h: (0, h)),
            pl.BlockSpec((V_DIM, META_PAD), lambda i, h: (h, 0)),
            lam_spec, lam_spec, lam_spec, lam_spec,
            pl.BlockSpec((V_DIM, 1), lambda i, h: (0, 0)),
        ],
        out_specs=pl.BlockSpec((1, V_DIM, s), lambda i, h: (i, h, 0)),
        out_shape=jax.ShapeDtypeStruct((b, d, s), _BF16),
        scratch_shapes=[pltpu.VMEM((2, META_PAD + (s if has_real else 0), 2 * tq), _F32)],
        compiler_params=pltpu.CompilerParams(
            dimension_semantics=("arbitrary", "arbitrary"),
            vmem_limit_bytes=V7X_VMEM_LIMIT_BYTES),
        name="diff_attn",
    )(qT, k, vT, k_meta, vT_meta, *lam_params, gsub_col)


def _ffn_T(hT, g_col, wguT_ref, wdT_ref):
    hn = _rmsnorm_rows(hT, g_col).astype(_BF16)
    acc = None
    for c0, c1 in FF_CHUNKS:
        g = _dot(wguT_ref[c0:c1, :], hn)
        u = _dot(wguT_ref[D_FF + c0:D_FF + c1, :], hn)
        a = (g * jax.nn.sigmoid(g) * u).astype(_BF16)
        part = _dot(wdT_ref[:, c0:c1], a)
        acc = part if acc is None else acc + part
    return acc


def _attn_out_ffn_kernel(x_ref, oT_ref, woT_ref, g_ref, wguT_ref, wdT_ref, hT_ref):
    h = x_ref[0].T + _dot(woT_ref[...], oT_ref[0])
    hT_ref[0] = h + _ffn_T(h, g_ref[...], wguT_ref, wdT_ref)


def _attn_out_ffn_call(x, oT, woT, g_col, wguT, wdT, tm):
    b, s, d = x.shape
    const = lambda shape: pl.BlockSpec(shape, lambda i, t: (0,) * len(shape),
                                       pipeline_mode=pl.Buffered(1))
    return pl.pallas_call(
        _attn_out_ffn_kernel,
        grid=(b, s // tm),
        in_specs=[
            pl.BlockSpec((1, tm, d), lambda i, t: (i, t, 0)),
            pl.BlockSpec((1, d, tm), lambda i, t: (i, 0, t)),
            const((d, d)), const((d, 1)), const((2 * D_FF, d)), const((d, D_FF)),
        ],
        out_specs=pl.BlockSpec((1, d, tm), lambda i, t: (i, 0, t)),
        out_shape=jax.ShapeDtypeStruct((b, d, s), _F32),
        compiler_params=pltpu.CompilerParams(
            dimension_semantics=("arbitrary", "arbitrary"),
            vmem_limit_bytes=V7X_VMEM_LIMIT_BYTES),
        name="attn_out_ffn",
    )(x, oT, woT, g_col, wguT, wdT)


def _conv_gate_T(hT, g_col, winT_ref):
    hn = _rmsnorm_rows(hT, g_col).astype(_BF16)
    bcu = _dot(winT_ref[...], hn)
    return bcu[:D_MODEL], bcu[D_MODEL:2 * D_MODEL] * bcu[2 * D_MODEL:]


def _conv_z_kernel(hT_ref, g_ref, winT_ref, zT_ref):
    _, z = _conv_gate_T(hT_ref[0], g_ref[...], winT_ref)
    zT_ref[0] = z


def _conv_ffn_kernel(hT_ref, halo_ref, g1_ref, winT_ref, cw_ref, woutT_ref, g2_ref, wguT_ref,
                     wdT_ref, out_ref, zprev_ref, *, tm):
    @pl.when(pl.program_id(1) == 0)
    def _():
        zprev_ref[...] = halo_ref[...]

    h = hT_ref[0]
    gate_b, z = _conv_gate_T(h, g1_ref[...], winT_ref)
    zext = jnp.concatenate([zprev_ref[...], z], axis=1)
    z1 = pltpu.roll(zext, 1, 1)[:, V7X_LANES:]
    z2 = pltpu.roll(zext, 2, 1)[:, V7X_LANES:]
    zc = cw_ref[2] * z + cw_ref[1] * z1 + cw_ref[0] * z2
    zprev_ref[...] = z[:, tm - V7X_LANES:]
    h = h + _dot(woutT_ref[...], (gate_b * zc).astype(_BF16))
    h = h + _ffn_T(h, g2_ref[...], wguT_ref, wdT_ref)
    out_ref[0] = h.T


def _conv_z_call(hT, g_col, winT):
    b, d, s = hT.shape
    return pl.pallas_call(
        _conv_z_kernel,
        grid=(b,),
        in_specs=[
            pl.BlockSpec((1, d, s), lambda i: (i, 0, 0)),
            pl.BlockSpec((d, 1), lambda i: (0, 0)),
            pl.BlockSpec((3 * d, d), lambda i: (0, 0)),
        ],
        out_specs=pl.BlockSpec((1, d, s), lambda i: (i, 0, 0)),
        out_shape=jax.ShapeDtypeStruct((b, d, s), _F32),
        compiler_params=pltpu.CompilerParams(
            dimension_semantics=("arbitrary",),
            vmem_limit_bytes=V7X_VMEM_LIMIT_BYTES),
        name="conv_z_meta",
    )(hT, g_col, winT)


def _conv_ffn_call(hT, halo, g1_col, winT, cw, woutT, g2_col, wguT, wdT, tm):
    b, d, s = hT.shape
    const = lambda shape: pl.BlockSpec(shape, lambda i, t: (0,) * len(shape),
                                       pipeline_mode=pl.Buffered(1))
    return pl.pallas_call(
        functools.partial(_conv_ffn_kernel, tm=tm),
        grid=(b, s // tm),
        in_specs=[
            pl.BlockSpec((1, d, tm), lambda i, t: (i, 0, t)),
            const((d, V7X_LANES)), const((d, 1)), const((3 * d, d)),
            const((CONV_WIDTH, d, 1)), const((d, d)), const((d, 1)),
            const((2 * D_FF, d)), const((d, D_FF)),
        ],
        out_specs=pl.BlockSpec((1, tm, d), lambda i, t: (i, t, 0)),
        out_shape=jax.ShapeDtypeStruct((b, s, d), _F32),
        scratch_shapes=[pltpu.VMEM((d, V7X_LANES), _F32)],
        compiler_params=pltpu.CompilerParams(
            dimension_semantics=("arbitrary", "arbitrary"),
            vmem_limit_bytes=V7X_VMEM_LIMIT_BYTES),
        name="conv_ffn",
    )(hT, halo, g1_col, winT, cw, woutT, g2_col, wguT, wdT)


def _rope_tables(gain, scale, pos0, n):
    half = HEAD_DIM // 2
    inv = 1.0 / (ROPE_THETA ** (jnp.arange(0, HEAD_DIM, 2, dtype=_F32) / HEAD_DIM))
    pos = jnp.arange(pos0, pos0 + n, dtype=_F32)
    ang = pos[None, :] * inv[:, None]
    cos, sin = jnp.cos(ang), jnp.sin(ang)
    g1 = gain[:half, None].astype(_F32) * scale
    g2 = gain[half:, None].astype(_F32) * scale
    return jnp.stack([g1 * cos, g2 * sin, g2 * cos, g1 * sin])


def _token_tile(s):
    for tm in (512, 256, 128):
        if s % tm == 0:
            return tm
    raise ValueError(f"sequence length {s} must be a multiple of 128")


def kernel(x, meta_tokens, mixer_norm_g, ffn_norm_g, attn_w_qkv, attn_q_gain, attn_k_gain, attn_lambda_q1, attn_lambda_k1, attn_lambda_q2, attn_lambda_k2, attn_sub_gain, attn_w_o, conv_w_in, conv_w, conv_w_out, ffn_w_gate_up, ffn_w_down):
    b, s, d = x.shape
    assert d == D_MODEL and meta_tokens.shape == (N_META, D_MODEL)
    assert mixer_norm_g.shape[0] == 2, "layer 0 = differential attention, layer 1 = short conv"
    tm = _token_tile(s)
    tq = 256 if s % 256 == 0 else 128

    wqkvT = attn_w_qkv[0].T.astype(_BF16)
    woT = attn_w_o[0].T.astype(_BF16)
    winT = conv_w_in[0].T.astype(_BF16)
    woutT = conv_w_out[0].T.astype(_BF16)
    wguT = [ffn_w_gate_up[i].T.astype(_BF16) for i in range(2)]
    wdT = [ffn_w_down[i].T.astype(_BF16) for i in range(2)]
    cw = conv_w[0].reshape(CONV_WIDTH, D_MODEL, 1).astype(_F32)
    g_mix0_row = mixer_norm_g[0].reshape(1, D_MODEL)
    g_mix1_col = mixer_norm_g[1].reshape(D_MODEL, 1)
    g_ffn_col = [ffn_norm_g[i].reshape(D_MODEL, 1) for i in range(2)]
    gsub_col = attn_sub_gain[0].reshape(V_DIM, 1)
    lam_params = [p[0].reshape(1, HEAD_DIM) for p in
                  (attn_lambda_q1, attn_lambda_k1, attn_lambda_q2, attn_lambda_k2)]
    q_scale = HEAD_DIM ** -0.5 * math.log2(math.e)

    x_meta = jnp.zeros((1, META_PAD, D_MODEL), x.dtype).at[0, :N_META].set(meta_tokens.astype(x.dtype))
    tabq_m = _rope_tables(attn_q_gain[0], q_scale, 0, META_PAD)
    tabk_m = _rope_tables(attn_k_gain[0], 1.0, 0, META_PAD)
    qT_m, k_m, vT_m = _qkv_call(x_meta, g_mix0_row, wqkvT, tabq_m, tabk_m, META_PAD)
    oT_m = _attn_call(qT_m, k_m, vT_m, k_m[0], vT_m[0], lam_params, gsub_col,
                      tq=META_PAD, pos_off=0, has_real=False)
    h1T_m = _attn_out_ffn_call(x_meta, oT_m, woT, g_ffn_col[0], wguT[0], wdT[0], META_PAD)
    zT_m = _conv_z_call(h1T_m, g_mix1_col, winT)
    halo = jnp.zeros((D_MODEL, V7X_LANES), _F32).at[:, V7X_LANES - 2:].set(
        zT_m[0][:, N_META - 2:N_META])

    tabq = _rope_tables(attn_q_gain[0], q_scale, N_META, s)
    tabk = _rope_tables(attn_k_gain[0], 1.0, N_META, s)
    qT, k, vT = _qkv_call(x, g_mix0_row, wqkvT, tabq, tabk, tm)
    oT = _attn_call(qT, k, vT, k_m[0], vT_m[0], lam_params, gsub_col,
                    tq=tq, pos_off=N_META, has_real=True)
    h1T = _attn_out_ffn_call(x, oT, woT, g_ffn_col[0], wguT[0], wdT[0], tm)
    return _conv_ffn_call(h1T, halo, g_mix1_col, winT, cw, woutT, g_ffn_col[1], wguT[1], wdT[1], tm)
```

```python
import functools
import math

import jax
import jax.numpy as jnp
from jax import lax
from jax.experimental import pallas as pl
from jax.experimental.pallas import tpu as pltpu

D_MODEL = 1024
N_META = 16
N_HEADS = 8
HEAD_DIM = 64
V_DIM = 2 * HEAD_DIM
D_FF = 2816
ROPE_THETA = 10000.0
CONV_WIDTH = 3
EPS = 1e-6
LAMBDA_INIT_L0 = 0.8 - 0.6

V7X_LANES = 128
META_PAD = V7X_LANES
V7X_VMEM_LIMIT_BYTES = 56 * 1024 * 1024
N_SCORE_BUFS = 3
MASKED = -1e30
FF_CHUNKS = ((0, 1024), (1024, 2048), (2048, D_FF))

_F32 = jnp.float32
_BF16 = jnp.bfloat16


def _dot(a, b):
    return jnp.dot(a, b, preferred_element_type=_F32)


def _rmsnorm_rows(hT, g_col):
    ms = jnp.mean(hT * hT, axis=0, keepdims=True)
    return hT * lax.rsqrt(ms + EPS) * g_col


def _qkv_kernel(x_ref, g_ref, wT_ref, tabq_ref, tabk_ref, qT_ref, k_ref, vT_ref):
    x = x_ref[0]
    ms = jnp.mean(x * x, axis=-1, keepdims=True)
    hn = (x * lax.rsqrt(ms + EPS) * g_ref[...]).astype(_BF16)
    def proj(sec):
        return lax.dot_general(wT_ref[sec * D_MODEL:(sec + 1) * D_MODEL, :], hn,
                               (((1,), (1,)), ((), ())), preferred_element_type=_F32)

    qf, kf, vf = proj(0), proj(1), proj(2)
    tabq = tabq_ref[...]
    tabk = tabk_ref[...]
    half = HEAD_DIM // 2

    def norm_rope(y, tab):
        yn = y * lax.rsqrt(jnp.mean(y * y, axis=0, keepdims=True) + EPS)
        x1, x2 = yn[:half], yn[half:]
        return jnp.concatenate([x1 * tab[0] - x2 * tab[1], x2 * tab[2] + x1 * tab[3]], axis=0)

    for i in range(2 * N_HEADS):
        r = i * HEAD_DIM
        qT_ref[0, r:r + HEAD_DIM, :] = norm_rope(qf[r:r + HEAD_DIM], tabq).astype(_BF16)
    for h in range(N_HEADS):
        r = h * V_DIM
        kh = jnp.concatenate([norm_rope(kf[r:r + HEAD_DIM], tabk),
                              norm_rope(kf[r + HEAD_DIM:r + V_DIM], tabk)], axis=0)
        k_ref[0, :, h * V_DIM:(h + 1) * V_DIM] = kh.T.astype(_BF16)
    vT_ref[0] = vf.astype(_BF16)


def _qkv_call(x, g_row, wT, tabq, tabk, tm):
    b, s, d = x.shape
    grid = (b, s // tm)
    const = lambda shape: pl.BlockSpec(shape, lambda i, t: (0,) * len(shape),
                                       pipeline_mode=pl.Buffered(1))
    return pl.pallas_call(
        _qkv_kernel,
        grid=grid,
        in_specs=[
            pl.BlockSpec((1, tm, d), lambda i, t: (i, t, 0)),
            const((1, d)),
            const((3 * d, d)),
            pl.BlockSpec((4, HEAD_DIM // 2, tm), lambda i, t: (0, 0, t)),
            pl.BlockSpec((4, HEAD_DIM // 2, tm), lambda i, t: (0, 0, t)),
        ],
        out_specs=[
            pl.BlockSpec((1, d, tm), lambda i, t: (i, 0, t)),
            pl.BlockSpec((1, tm, d), lambda i, t: (i, t, 0)),
            pl.BlockSpec((1, d, tm), lambda i, t: (i, 0, t)),
        ],
        out_shape=[
            jax.ShapeDtypeStruct((b, d, s), _BF16),
            jax.ShapeDtypeStruct((b, s, d), _BF16),
            jax.ShapeDtypeStruct((b, d, s), _BF16),
        ],
        compiler_params=pltpu.CompilerParams(
            dimension_semantics=("arbitrary", "arbitrary"),
            vmem_limit_bytes=V7X_VMEM_LIMIT_BYTES),
        name="qkv_proj",
    )(x, g_row, wT, tabq, tabk)


def _attn_kernel(qT_ref, k_ref, vT_ref, km_ref, vmT_ref, lq1_ref, lk1_ref, lq2_ref, lk2_ref,
                 gsub_ref, oT_ref, s_ref, p_ref, *, n_qblk, tq, pos_off, has_real):
    lam = (jnp.exp(jnp.sum(lq1_ref[...] * lk1_ref[...], keepdims=True))
           - jnp.exp(jnp.sum(lq2_ref[...] * lk2_ref[...], keepdims=True))
           + LAMBDA_INIT_L0)
    zero_half = jnp.zeros((HEAD_DIM, tq), _BF16)
    key_row = lax.broadcasted_iota(jnp.int32, (tq, 2 * tq), 0)
    qry_col = lax.broadcasted_iota(jnp.int32, (tq, 2 * tq), 1) % tq
    diag_mask = key_row <= qry_col
    meta_row = lax.broadcasted_iota(jnp.int32, (N_META, 2 * tq), 0)
    meta_col = lax.broadcasted_iota(jnp.int32, (N_META, 2 * tq), 1) % tq
    meta_zero_rows = jnp.zeros((META_PAD - N_META, tq), _BF16)

    def col_reduce(fn, v):
        return fn(v.reshape(v.shape[0] // 8, 8, v.shape[1]), axis=0)

    def query_block(qi, slot):
        qT = qT_ref[0, :, qi * tq:(qi + 1) * tq]
        qab = jnp.concatenate([jnp.concatenate([qT[:HEAD_DIM], zero_half], axis=0),
                               jnp.concatenate([zero_half, qT[HEAD_DIM:]], axis=0)], axis=1)
        q0 = pos_off + qi * tq
        meta_mask = None if q0 >= N_META - 1 else meta_row <= meta_col + q0
        chunks = [(0, N_META, km_ref[0:N_META, :], vmT_ref[...], meta_mask)]
        if has_real:
            for j in range(qi + 1):
                chunks.append((N_META + j * tq, tq, k_ref[0, j * tq:(j + 1) * tq, :],
                               vT_ref[0, :, j * tq:(j + 1) * tq], diag_mask if j == qi else None))
        s_buf = s_ref.at[slot]
        p_buf = p_ref.at[slot]
        m8 = jnp.full((8, 2 * tq), MASKED, _F32)
        for r0, nr, kc, _, mask in chunks:
            s = _dot(kc, qab)
            if mask is not None:
                s = jnp.where(mask, s, MASKED)
            s_buf[r0:r0 + nr, :] = s
            m8 = jnp.maximum(m8, col_reduce(jnp.max, s))
            yield False
        m = jnp.max(m8, axis=0, keepdims=True)
        yield True
        l8 = jnp.zeros((8, 2 * tq), _F32)
        for r0, nr, _, _, _ in chunks:
            p = jnp.exp2(s_buf[r0:r0 + nr, :] - m)
            l8 = l8 + col_reduce(jnp.sum, p)
            p_buf[r0:r0 + nr, :] = p.astype(_BF16)
            yield False
        l = jnp.sum(l8, axis=0, keepdims=True)
        l1, l2 = l[:, :tq], l[:, tq:]
        rho = (lam * l1 / l2).astype(_BF16)
        yield True
        acc = jnp.zeros((V_DIM, tq), _F32)
        for r0, nr, _, vc, _ in chunks:
            pb = p_buf[r0:r0 + nr, :]
            a = pb[:, :tq] - rho * pb[:, tq:]
            if nr == N_META:
                a = jnp.concatenate([a, meta_zero_rows], axis=0)
            acc = acc + _dot(vc, a)
            yield False
        on = _rmsnorm_rows(acc * (1.0 / l1), gsub_ref[...]) * (1.0 - LAMBDA_INIT_L0)
        oT_ref[0, :, qi * tq:(qi + 1) * tq] = on.astype(_BF16)
        yield True

    order = list(range(1, n_qblk, 2)) + list(range(n_qblk - 1 - (n_qblk - 1) % 2, -1, -2))
    active = []
    for stage in range(n_qblk + 2):
        if stage < n_qblk:
            active.append(query_block(order[stage], stage % N_SCORE_BUFS))
        running = list(active)
        while running:
            running = [g for g in running if not next(g)]
        if stage >= 2:
            active.pop(0)


def _attn_call(qT, k, vT, k_meta, vT_meta, lam_params, gsub_col, *, tq, pos_off, has_real):
    b, d, s = qT.shape
    kernel = functools.partial(_attn_kernel, n_qblk=s // tq, tq=tq, pos_off=pos_off,
                               has_real=has_real)
    lam_spec = pl.BlockSpec((1, HEAD_DIM), lambda i, h: (0, 0))
    return pl.pallas_call(
        kernel,
        grid=(b, N_HEADS),
        in_specs=[
            pl.BlockSpec((1, V_DIM, s), lambda i, h: (i, h, 0)),
            pl.BlockSpec((1, s, V_DIM), lambda i, h: (i, 0, h)),
            pl.BlockSpec((1, V_DIM, s), lambda i, h: (i, h, 0)),
            pl.BlockSpec((META_PAD, V_DIM), lambda i, h: (0, h)),
            pl.BlockSpec((V_DIM, META_PAD), lambda i, h: (h, 0)),
            lam_spec, lam_spec, lam_spec, lam_spec,
            pl.BlockSpec((V_DIM, 1), lambda i, h: (0, 0)),
        ],
        out_specs=pl.BlockSpec((1, V_DIM, s), lambda i, h: (i, h, 0)),
        out_shape=jax.ShapeDtypeStruct((b, d, s), _BF16),
        scratch_shapes=[pltpu.VMEM((N_SCORE_BUFS, N_META + (s if has_real else 0), 2 * tq), _F32),
                        pltpu.VMEM((N_SCORE_BUFS, N_META + (s if has_real else 0), 2 * tq), _BF16)],
        compiler_params=pltpu.CompilerParams(
            dimension_semantics=("arbitrary", "arbitrary"),
            vmem_limit_bytes=V7X_VMEM_LIMIT_BYTES),
        name="diff_attn",
    )(qT, k, vT, k_meta, vT_meta, *lam_params, gsub_col)


def _ffn_T(hT, g_col, wguT_ref, wdT_ref):
    hn = _rmsnorm_rows(hT, g_col).astype(_BF16)
    acc = None
    for c0, c1 in FF_CHUNKS:
        g = _dot(wguT_ref[c0:c1, :], hn)
        u = _dot(wguT_ref[D_FF + c0:D_FF + c1, :], hn)
        a = (g * jax.nn.sigmoid(g) * u).astype(_BF16)
        part = _dot(wdT_ref[:, c0:c1], a)
        acc = part if acc is None else acc + part
    return acc


def _attn_out_ffn_kernel(x_ref, oT_ref, woT_ref, g_ref, wguT_ref, wdT_ref, hT_ref):
    h = x_ref[0].T + _dot(woT_ref[...], oT_ref[0])
    hT_ref[0] = h + _ffn_T(h, g_ref[...], wguT_ref, wdT_ref)


def _attn_out_ffn_call(x, oT, woT, g_col, wguT, wdT, tm):
    b, s, d = x.shape
    const = lambda shape: pl.BlockSpec(shape, lambda i, t: (0,) * len(shape),
                                       pipeline_mode=pl.Buffered(1))
    return pl.pallas_call(
        _attn_out_ffn_kernel,
        grid=(b, s // tm),
        in_specs=[
            pl.BlockSpec((1, tm, d), lambda i, t: (i, t, 0)),
            pl.BlockSpec((1, d, tm), lambda i, t: (i, 0, t)),
            const((d, d)), const((d, 1)), const((2 * D_FF, d)), const((d, D_FF)),
        ],
        out_specs=pl.BlockSpec((1, d, tm), lambda i, t: (i, 0, t)),
        out_shape=jax.ShapeDtypeStruct((b, d, s), _F32),
        compiler_params=pltpu.CompilerParams(
            dimension_semantics=("arbitrary", "arbitrary"),
            vmem_limit_bytes=V7X_VMEM_LIMIT_BYTES),
        name="attn_out_ffn",
    )(x, oT, woT, g_col, wguT, wdT)


def _conv_gate_T(hT, g_col, winT_ref):
    hn = _rmsnorm_rows(hT, g_col).astype(_BF16)
    bcu = _dot(winT_ref[...], hn)
    return bcu[:D_MODEL], bcu[D_MODEL:2 * D_MODEL] * bcu[2 * D_MODEL:]


def _conv_z_kernel(hT_ref, g_ref, winT_ref, zT_ref):
    _, z = _conv_gate_T(hT_ref[0], g_ref[...], winT_ref)
    zT_ref[0] = z


def _conv_ffn_kernel(hT_ref, halo_ref, g1_ref, winT_ref, cw_ref, woutT_ref, g2_ref, wguT_ref,
                     wdT_ref, out_ref, zprev_ref, *, tm):
    @pl.when(pl.program_id(1) == 0)
    def _():
        zprev_ref[...] = halo_ref[...]

    h = hT_ref[0]
    gate_b, z = _conv_gate_T(h, g1_ref[...], winT_ref)
    zext = jnp.concatenate([zprev_ref[...], z], axis=1)
    z1 = pltpu.roll(zext, 1, 1)[:, V7X_LANES:]
    z2 = pltpu.roll(zext, 2, 1)[:, V7X_LANES:]
    zc = cw_ref[2] * z + cw_ref[1] * z1 + cw_ref[0] * z2
    zprev_ref[...] = z[:, tm - V7X_LANES:]
    h = h + _dot(woutT_ref[...], (gate_b * zc).astype(_BF16))
    h = h + _ffn_T(h, g2_ref[...], wguT_ref, wdT_ref)
    out_ref[0] = h.T


def _conv_z_call(hT, g_col, winT):
    b, d, s = hT.shape
    return pl.pallas_call(
        _conv_z_kernel,
        grid=(b,),
        in_specs=[
            pl.BlockSpec((1, d, s), lambda i: (i, 0, 0)),
            pl.BlockSpec((d, 1), lambda i: (0, 0)),
            pl.BlockSpec((3 * d, d), lambda i: (0, 0)),
        ],
        out_specs=pl.BlockSpec((1, d, s), lambda i: (i, 0, 0)),
        out_shape=jax.ShapeDtypeStruct((b, d, s), _F32),
        compiler_params=pltpu.CompilerParams(
            dimension_semantics=("arbitrary",),
            vmem_limit_bytes=V7X_VMEM_LIMIT_BYTES),
        name="conv_z_meta",
    )(hT, g_col, winT)


def _conv_ffn_call(hT, halo, g1_col, winT, cw, woutT, g2_col, wguT, wdT, tm):
    b, d, s = hT.shape
    const = lambda shape: pl.BlockSpec(shape, lambda i, t: (0,) * len(shape),
                                       pipeline_mode=pl.Buffered(1))
    return pl.pallas_call(
        functools.partial(_conv_ffn_kernel, tm=tm),
        grid=(b, s // tm),
        in_specs=[
            pl.BlockSpec((1, d, tm), lambda i, t: (i, 0, t)),
            const((d, V7X_LANES)), const((d, 1)), const((3 * d, d)),
            const((CONV_WIDTH, d, 1)), const((d, d)), const((d, 1)),
            const((2 * D_FF, d)), const((d, D_FF)),
        ],
        out_specs=pl.BlockSpec((1, tm, d), lambda i, t: (i, t, 0)),
        out_shape=jax.ShapeDtypeStruct((b, s, d), _F32),
        scratch_shapes=[pltpu.VMEM((d, V7X_LANES), _F32)],
        compiler_params=pltpu.CompilerParams(
            dimension_semantics=("arbitrary", "arbitrary"),
            vmem_limit_bytes=V7X_VMEM_LIMIT_BYTES),
        name="conv_ffn",
    )(hT, halo, g1_col, winT, cw, woutT, g2_col, wguT, wdT)


def _rope_tables(gain, scale, pos0, n):
    half = HEAD_DIM // 2
    inv = 1.0 / (ROPE_THETA ** (jnp.arange(0, HEAD_DIM, 2, dtype=_F32) / HEAD_DIM))
    pos = jnp.arange(pos0, pos0 + n, dtype=_F32)
    ang = pos[None, :] * inv[:, None]
    cos, sin = jnp.cos(ang), jnp.sin(ang)
    g1 = gain[:half, None].astype(_F32) * scale
    g2 = gain[half:, None].astype(_F32) * scale
    return jnp.stack([g1 * cos, g2 * sin, g2 * cos, g1 * sin])


def _token_tile(s):
    for tm in (512, 256, 128):
        if s % tm == 0:
            return tm
    raise ValueError(f"sequence length {s} must be a multiple of 128")


def kernel(x, meta_tokens, mixer_norm_g, ffn_norm_g, attn_w_qkv, attn_q_gain, attn_k_gain, attn_lambda_q1, attn_lambda_k1, attn_lambda_q2, attn_lambda_k2, attn_sub_gain, attn_w_o, conv_w_in, conv_w, conv_w_out, ffn_w_gate_up, ffn_w_down):
    b, s, d = x.shape
    assert d == D_MODEL and meta_tokens.shape == (N_META, D_MODEL)
    assert mixer_norm_g.shape[0] == 2, "layer 0 = differential attention, layer 1 = short conv"
    tm = _token_tile(s)
    tq = 256 if s % 256 == 0 else 128

    wqkvT = attn_w_qkv[0].T.astype(_BF16)
    woT = attn_w_o[0].T.astype(_BF16)
    winT = conv_w_in[0].T.astype(_BF16)
    woutT = conv_w_out[0].T.astype(_BF16)
    wguT = [ffn_w_gate_up[i].T.astype(_BF16) for i in range(2)]
    wdT = [ffn_w_down[i].T.astype(_BF16) for i in range(2)]
    cw = conv_w[0].reshape(CONV_WIDTH, D_MODEL, 1).astype(_F32)
    g_mix0_row = mixer_norm_g[0].reshape(1, D_MODEL)
    g_mix1_col = mixer_norm_g[1].reshape(D_MODEL, 1)
    g_ffn_col = [ffn_norm_g[i].reshape(D_MODEL, 1) for i in range(2)]
    gsub_col = attn_sub_gain[0].reshape(V_DIM, 1)
    lam_params = [p[0].reshape(1, HEAD_DIM) for p in
                  (attn_lambda_q1, attn_lambda_k1, attn_lambda_q2, attn_lambda_k2)]
    q_scale = HEAD_DIM ** -0.5 * math.log2(math.e)

    x_meta = jnp.zeros((1, META_PAD, D_MODEL), x.dtype).at[0, :N_META].set(meta_tokens.astype(x.dtype))
    tabq_m = _rope_tables(attn_q_gain[0], q_scale, 0, META_PAD)
    tabk_m = _rope_tables(attn_k_gain[0], 1.0, 0, META_PAD)
    qT_m, k_m, vT_m = _qkv_call(x_meta, g_mix0_row, wqkvT, tabq_m, tabk_m, META_PAD)
    oT_m = _attn_call(qT_m, k_m, vT_m, k_m[0], vT_m[0], lam_params, gsub_col,
                      tq=META_PAD, pos_off=0, has_real=False)
    h1T_m = _attn_out_ffn_call(x_meta, oT_m, woT, g_ffn_col[0], wguT[0], wdT[0], META_PAD)
    zT_m = _conv_z_call(h1T_m, g_mix1_col, winT)
    halo = jnp.zeros((D_MODEL, V7X_LANES), _F32).at[:, V7X_LANES - 2:].set(
        zT_m[0][:, N_META - 2:N_META])

    tabq = _rope_tables(attn_q_gain[0], q_scale, N_META, s)
    tabk = _rope_tables(attn_k_gain[0], 1.0, N_META, s)
    qT, k, vT = _qkv_call(x, g_mix0_row, wqkvT, tabq, tabk, tm)
    oT = _attn_call(qT, k, vT, k_m[0], vT_m[0], lam_params, gsub_col,
                    tq=tq, pos_off=N_META, has_real=True)
    h1T = _attn_out_ffn_call(x, oT, woT, g_ffn_col[0], wguT[0], wdT[0], tm)
    return _conv_ffn_call(h1T, halo, g_mix1_col, winT, cw, woutT, g_ffn_col[1], wguT[1], wdT[1], tm)
```

```python
import functools
import math

import jax
import jax.numpy as jnp
from jax import lax
from jax.experimental import pallas as pl
from jax.experimental.pallas import tpu as pltpu

D_MODEL = 1024
N_META = 16
N_HEADS = 8
HEAD_DIM = 64
V_DIM = 2 * HEAD_DIM
D_FF = 2816
ROPE_THETA = 10000.0
CONV_WIDTH = 3
EPS = 1e-6
LAMBDA_INIT_L0 = 0.8 - 0.6

V7X_LANES = 128
META_PAD = V7X_LANES
V7X_VMEM_LIMIT_BYTES = 56 * 1024 * 1024
N_SCORE_BUFS = 3
FAST_ATTN_IN_FLIGHT = 2
FAST_SOFTMAX_MAX_SHIFT = 50.0
MASKED = -1e30
FF_CHUNKS = ((0, 1024), (1024, 2048), (2048, D_FF))

_F32 = jnp.float32
_BF16 = jnp.bfloat16


def _dot(a, b):
    return jnp.dot(a, b, preferred_element_type=_F32)


def _rmsnorm_rows(hT, g_col):
    ms = jnp.mean(hT * hT, axis=0, keepdims=True)
    return hT * lax.rsqrt(ms + EPS) * g_col


def _qkv_kernel(x_ref, g_ref, wT_ref, tabq_ref, tabk_ref, qT_ref, k_ref, vT_ref):
    x = x_ref[0]
    ms = jnp.mean(x * x, axis=-1, keepdims=True)
    hn = (x * lax.rsqrt(ms + EPS) * g_ref[...]).astype(_BF16)
    def proj(sec):
        return lax.dot_general(wT_ref[sec * D_MODEL:(sec + 1) * D_MODEL, :], hn,
                               (((1,), (1,)), ((), ())), preferred_element_type=_F32)

    qf, kf, vf = proj(0), proj(1), proj(2)
    tabq = tabq_ref[...]
    tabk = tabk_ref[...]
    half = HEAD_DIM // 2

    def norm_rope(y, tab):
        yn = y * lax.rsqrt(jnp.mean(y * y, axis=0, keepdims=True) + EPS)
        x1, x2 = yn[:half], yn[half:]
        return jnp.concatenate([x1 * tab[0] - x2 * tab[1], x2 * tab[2] + x1 * tab[3]], axis=0)

    for i in range(2 * N_HEADS):
        r = i * HEAD_DIM
        qT_ref[0, r:r + HEAD_DIM, :] = norm_rope(qf[r:r + HEAD_DIM], tabq).astype(_BF16)
    for h in range(N_HEADS):
        r = h * V_DIM
        kh = jnp.concatenate([norm_rope(kf[r:r + HEAD_DIM], tabk),
                              norm_rope(kf[r + HEAD_DIM:r + V_DIM], tabk)], axis=0)
        k_ref[0, :, h * V_DIM:(h + 1) * V_DIM] = kh.T.astype(_BF16)
    vT_ref[0] = vf.astype(_BF16)


def _qkv_call(x, g_row, wT, tabq, tabk, tm):
    b, s, d = x.shape
    grid = (b, s // tm)
    const = lambda shape: pl.BlockSpec(shape, lambda i, t: (0,) * len(shape),
                                       pipeline_mode=pl.Buffered(1))
    return pl.pallas_call(
        _qkv_kernel,
        grid=grid,
        in_specs=[
            pl.BlockSpec((1, tm, d), lambda i, t: (i, t, 0)),
            const((1, d)),
            const((3 * d, d)),
            pl.BlockSpec((4, HEAD_DIM // 2, tm), lambda i, t: (0, 0, t)),
            pl.BlockSpec((4, HEAD_DIM // 2, tm), lambda i, t: (0, 0, t)),
        ],
        out_specs=[
            pl.BlockSpec((1, d, tm), lambda i, t: (i, 0, t)),
            pl.BlockSpec((1, tm, d), lambda i, t: (i, t, 0)),
            pl.BlockSpec((1, d, tm), lambda i, t: (i, 0, t)),
        ],
        out_shape=[
            jax.ShapeDtypeStruct((b, d, s), _BF16),
            jax.ShapeDtypeStruct((b, s, d), _BF16),
            jax.ShapeDtypeStruct((b, d, s), _BF16),
        ],
        compiler_params=pltpu.CompilerParams(
            dimension_semantics=("arbitrary", "arbitrary"),
            vmem_limit_bytes=V7X_VMEM_LIMIT_BYTES),
        name="qkv_proj",
    )(x, g_row, wT, tabq, tabk)


def _attn_kernel(qT_ref, k_ref, vT_ref, km_ref, vmT_ref, lq1_ref, lk1_ref, lq2_ref, lk2_ref,
                 gsub_ref, oT_ref, s_ref, p_ref, *, n_qblk, tq, pos_off, has_real):
    lam = (jnp.exp(jnp.sum(lq1_ref[...] * lk1_ref[...], keepdims=True))
           - jnp.exp(jnp.sum(lq2_ref[...] * lk2_ref[...], keepdims=True))
           + LAMBDA_INIT_L0)
    zero_half = jnp.zeros((HEAD_DIM, tq), _BF16)
    key_row = lax.broadcasted_iota(jnp.int32, (tq, 2 * tq), 0)
    qry_col = lax.broadcasted_iota(jnp.int32, (tq, 2 * tq), 1) % tq
    diag_mask = key_row <= qry_col
    meta_row = lax.broadcasted_iota(jnp.int32, (N_META, 2 * tq), 0)
    meta_col = lax.broadcasted_iota(jnp.int32, (N_META, 2 * tq), 1) % tq
    meta_zero_rows = jnp.zeros((META_PAD - N_META, tq), _BF16)

    def col_reduce(fn, v):
        return fn(v.reshape(v.shape[0] // 8, 8, v.shape[1]), axis=0)

    def query_block(qi, slot):
        qT = qT_ref[0, :, qi * tq:(qi + 1) * tq]
        qab = jnp.concatenate([jnp.concatenate([qT[:HEAD_DIM], zero_half], axis=0),
                               jnp.concatenate([zero_half, qT[HEAD_DIM:]], axis=0)], axis=1)
        q0 = pos_off + qi * tq
        meta_mask = None if q0 >= N_META - 1 else meta_row <= meta_col + q0
        chunks = [(0, N_META, km_ref[0:N_META, :], vmT_ref[...], meta_mask)]
        if has_real:
            for j in range(qi + 1):
                chunks.append((N_META + j * tq, tq, k_ref[0, j * tq:(j + 1) * tq, :],
                               vT_ref[0, :, j * tq:(j + 1) * tq], diag_mask if j == qi else None))
        s_buf = s_ref.at[slot]
        p_buf = p_ref.at[slot]
        m8 = jnp.full((8, 2 * tq), MASKED, _F32)
        for r0, nr, kc, _, mask in chunks:
            s = _dot(kc, qab)
            if mask is not None:
                s = jnp.where(mask, s, MASKED)
            s_buf[r0:r0 + nr, :] = s
            m8 = jnp.maximum(m8, col_reduce(jnp.max, s))
            yield False
        m = jnp.max(m8, axis=0, keepdims=True)
        yield True
        l8 = jnp.zeros((8, 2 * tq), _F32)
        for r0, nr, _, _, _ in chunks:
            p = jnp.exp2(s_buf[r0:r0 + nr, :] - m)
            l8 = l8 + col_reduce(jnp.sum, p)
            p_buf[r0:r0 + nr, :] = p.astype(_BF16)
            yield False
        l = jnp.sum(l8, axis=0, keepdims=True)
        l1, l2 = l[:, :tq], l[:, tq:]
        rho = (lam * l1 / l2).astype(_BF16)
        yield True
        acc = jnp.zeros((V_DIM, tq), _F32)
        for r0, nr, _, vc, _ in chunks:
            pb = p_buf[r0:r0 + nr, :]
            a = pb[:, :tq] - rho * pb[:, tq:]
            if nr == N_META:
                a = jnp.concatenate([a, meta_zero_rows], axis=0)
            acc = acc + _dot(vc, a)
            yield False
        on = _rmsnorm_rows(acc * (1.0 / l1), gsub_ref[...]) * (1.0 - LAMBDA_INIT_L0)
        oT_ref[0, :, qi * tq:(qi + 1) * tq] = on.astype(_BF16)
        yield True

    order = list(range(1, n_qblk, 2)) + list(range(n_qblk - 1 - (n_qblk - 1) % 2, -1, -2))
    active = []
    for stage in range(n_qblk + 2):
        if stage < n_qblk:
            active.append(query_block(order[stage], stage % N_SCORE_BUFS))
        running = list(active)
        while running:
            running = [g for g in running if not next(g)]
        if stage >= 2:
            active.pop(0)


def _attn_fast_kernel(qT_ref, k_ref, vT_ref, km_ref, vmT_ref, lq1_ref, lk1_ref, lq2_ref, lk2_ref,
                      gsub_ref, shift_ref, oT_ref, *, n_qblk, tq):
    lam = (jnp.exp(jnp.sum(lq1_ref[...] * lk1_ref[...], keepdims=True))
           - jnp.exp(jnp.sum(lq2_ref[...] * lk2_ref[...], keepdims=True))
           + LAMBDA_INIT_L0)
    shift = shift_ref[...]
    zero_half = jnp.zeros((HEAD_DIM, tq), _BF16)
    key_row = lax.broadcasted_iota(jnp.int32, (tq, 2 * tq), 0)
    qry_col = lax.broadcasted_iota(jnp.int32, (tq, 2 * tq), 1) % tq
    diag_mask = key_row <= qry_col
    meta_zero_rows = jnp.zeros((META_PAD - N_META, tq), _BF16)

    def col_sum(v):
        return jnp.sum(v.reshape(v.shape[0] // 8, 8, v.shape[1]), axis=0)

    def query_block(qi):
        qT = qT_ref[0, :, qi * tq:(qi + 1) * tq]
        qab = jnp.concatenate([jnp.concatenate([qT[:HEAD_DIM], zero_half], axis=0),
                               jnp.concatenate([zero_half, qT[HEAD_DIM:]], axis=0)], axis=1)
        chunks = [(km_ref[0:N_META, :], vmT_ref[...], None)]
        for j in range(qi + 1):
            chunks.append((k_ref[0, j * tq:(j + 1) * tq, :], vT_ref[0, :, j * tq:(j + 1) * tq],
                           diag_mask if j == qi else None))
        l8 = jnp.zeros((8, 2 * tq), _F32)
        acc1 = jnp.zeros((V_DIM, tq), _F32)
        acc2 = jnp.zeros((V_DIM, tq), _F32)
        s_next = _dot(chunks[0][0], qab)
        yield
        for ci, (kc, vc, mask) in enumerate(chunks):
            s = s_next
            if ci + 1 < len(chunks):
                s_next = _dot(chunks[ci + 1][0], qab)
                yield
            if mask is not None:
                s = jnp.where(mask, s, MASKED)
            p = jnp.exp2(s - shift)
            l8 = l8 + col_sum(p)
            pb = p.astype(_BF16)
            p1, p2 = pb[:, :tq], pb[:, tq:]
            if kc.shape[0] == N_META:
                p1 = jnp.concatenate([p1, meta_zero_rows], axis=0)
                p2 = jnp.concatenate([p2, meta_zero_rows], axis=0)
            acc1 = acc1 + _dot(vc, p1)
            acc2 = acc2 + _dot(vc, p2)
            yield
        l = jnp.sum(l8, axis=0, keepdims=True)
        o = acc1 * (1.0 / l[:, :tq]) - acc2 * (lam / l[:, tq:])
        on = _rmsnorm_rows(o, gsub_ref[...]) * (1.0 - LAMBDA_INIT_L0)
        oT_ref[0, :, qi * tq:(qi + 1) * tq] = on.astype(_BF16)

    pending = [query_block(qi) for qi in range(n_qblk - 1, -1, -1)]
    active = []
    while pending or active:
        while pending and len(active) < FAST_ATTN_IN_FLIGHT:
            active.append(pending.pop(0))
        for g in list(active):
            try:
                next(g)
            except StopIteration:
                active.remove(g)


def _attn_fast_call(qT, k, vT, k_meta, vT_meta, lam_params, gsub_col, shift, *, tq):
    b, d, s = qT.shape
    kernel = functools.partial(_attn_fast_kernel, n_qblk=s // tq, tq=tq)
    lam_spec = pl.BlockSpec((1, HEAD_DIM), lambda i, h: (0, 0))
    return pl.pallas_call(
        kernel,
        grid=(b, N_HEADS),
        in_specs=[
            pl.BlockSpec((1, V_DIM, s), lambda i, h: (i, h, 0)),
            pl.BlockSpec((1, s, V_DIM), lambda i, h: (i, 0, h)),
            pl.BlockSpec((1, V_DIM, s), lambda i, h: (i, h, 0)),
            pl.BlockSpec((META_PAD, V_DIM), lambda i, h: (0, h)),
            pl.BlockSpec((V_DIM, META_PAD), lambda i, h: (h, 0)),
            lam_spec, lam_spec, lam_spec, lam_spec,
            pl.BlockSpec((V_DIM, 1), lambda i, h: (0, 0)),
            pl.BlockSpec((1, 1), lambda i, h: (0, 0)),
        ],
        out_specs=pl.BlockSpec((1, V_DIM, s), lambda i, h: (i, h, 0)),
        out_shape=jax.ShapeDtypeStruct((b, d, s), _BF16),
        compiler_params=pltpu.CompilerParams(
            dimension_semantics=("arbitrary", "arbitrary"),
            vmem_limit_bytes=V7X_VMEM_LIMIT_BYTES),
        name="diff_attn_fast",
    )(qT, k, vT, k_meta, vT_meta, *lam_params, gsub_col, shift)


def _attn_call(qT, k, vT, k_meta, vT_meta, lam_params, gsub_col, *, tq, pos_off, has_real):
    b, d, s = qT.shape
    kernel = functools.partial(_attn_kernel, n_qblk=s // tq, tq=tq, pos_off=pos_off,
                               has_real=has_real)
    lam_spec = pl.BlockSpec((1, HEAD_DIM), lambda i, h: (0, 0))
    return pl.pallas_call(
        kernel,
        grid=(b, N_HEADS),
        in_specs=[
            pl.BlockSpec((1, V_DIM, s), lambda i, h: (i, h, 0)),
            pl.BlockSpec((1, s, V_DIM), lambda i, h: (i, 0, h)),
            pl.BlockSpec((1, V_DIM, s), lambda i, h: (i, h, 0)),
            pl.BlockSpec((META_PAD, V_DIM), lambda i, h: (0, h)),
            pl.BlockSpec((V_DIM, META_PAD), lambda i, h: (h, 0)),
            lam_spec, lam_spec, lam_spec, lam_spec,
            pl.BlockSpec((V_DIM, 1), lambda i, h: (0, 0)),
        ],
        out_specs=pl.BlockSpec((1, V_DIM, s), lambda i, h: (i, h, 0)),
        out_shape=jax.ShapeDtypeStruct((b, d, s), _BF16),
        scratch_shapes=[pltpu.VMEM((N_SCORE_BUFS, N_META + (s if has_real else 0), 2 * tq), _F32),
                        pltpu.VMEM((N_SCORE_BUFS, N_META + (s if has_real else 0), 2 * tq), _BF16)],
        compiler_params=pltpu.CompilerParams(
            dimension_semantics=("arbitrary", "arbitrary"),
            vmem_limit_bytes=V7X_VMEM_LIMIT_BYTES),
        name="diff_attn",
    )(qT, k, vT, k_meta, vT_meta, *lam_params, gsub_col)


def _ffn_T(hT, g_col, wguT_ref, wdT_ref):
    hn = _rmsnorm_rows(hT, g_col).astype(_BF16)
    acc = None
    for c0, c1 in FF_CHUNKS:
        g = _dot(wguT_ref[c0:c1, :], hn)
        u = _dot(wguT_ref[D_FF + c0:D_FF + c1, :], hn)
        a = (g * jax.nn.sigmoid(g) * u).astype(_BF16)
        part = _dot(wdT_ref[:, c0:c1], a)
        acc = part if acc is None else acc + part
    return acc


def _attn_out_ffn_kernel(x_ref, oT_ref, woT_ref, g_ref, wguT_ref, wdT_ref, hT_ref):
    h = x_ref[0].T + _dot(woT_ref[...], oT_ref[0])
    hT_ref[0] = h + _ffn_T(h, g_ref[...], wguT_ref, wdT_ref)


def _attn_out_ffn_call(x, oT, woT, g_col, wguT, wdT, tm):
    b, s, d = x.shape
    const = lambda shape: pl.BlockSpec(shape, lambda i, t: (0,) * len(shape),
                                       pipeline_mode=pl.Buffered(1))
    return pl.pallas_call(
        _attn_out_ffn_kernel,
        grid=(b, s // tm),
        in_specs=[
            pl.BlockSpec((1, tm, d), lambda i, t: (i, t, 0)),
            pl.BlockSpec((1, d, tm), lambda i, t: (i, 0, t)),
            const((d, d)), const((d, 1)), const((2 * D_FF, d)), const((d, D_FF)),
        ],
        out_specs=pl.BlockSpec((1, d, tm), lambda i, t: (i, 0, t)),
        out_shape=jax.ShapeDtypeStruct((b, d, s), _F32),
        compiler_params=pltpu.CompilerParams(
            dimension_semantics=("arbitrary", "arbitrary"),
            vmem_limit_bytes=V7X_VMEM_LIMIT_BYTES),
        name="attn_out_ffn",
    )(x, oT, woT, g_col, wguT, wdT)


def _conv_gate_T(hT, g_col, winT_ref):
    hn = _rmsnorm_rows(hT, g_col).astype(_BF16)
    bcu = _dot(winT_ref[...], hn)
    return bcu[:D_MODEL], bcu[D_MODEL:2 * D_MODEL] * bcu[2 * D_MODEL:]


def _conv_z_kernel(hT_ref, g_ref, winT_ref, zT_ref):
    _, z = _conv_gate_T(hT_ref[0], g_ref[...], winT_ref)
    zT_ref[0] = z


def _conv_ffn_kernel(hT_ref, halo_ref, g1_ref, winT_ref, cw_ref, woutT_ref, g2_ref, wguT_ref,
                     wdT_ref, out_ref, zprev_ref, *, tm):
    @pl.when(pl.program_id(1) == 0)
    def _():
        zprev_ref[...] = halo_ref[...]

    h = hT_ref[0]
    gate_b, z = _conv_gate_T(h, g1_ref[...], winT_ref)
    zext = jnp.concatenate([zprev_ref[...], z], axis=1)
    z1 = pltpu.roll(zext, 1, 1)[:, V7X_LANES:]
    z2 = pltpu.roll(zext, 2, 1)[:, V7X_LANES:]
    zc = cw_ref[2] * z + cw_ref[1] * z1 + cw_ref[0] * z2
    zprev_ref[...] = z[:, tm - V7X_LANES:]
    h = h + _dot(woutT_ref[...], (gate_b * zc).astype(_BF16))
    h = h + _ffn_T(h, g2_ref[...], wguT_ref, wdT_ref)
    out_ref[0] = h.T


def _conv_z_call(hT, g_col, winT):
    b, d, s = hT.shape
    return pl.pallas_call(
        _conv_z_kernel,
        grid=(b,),
        in_specs=[
            pl.BlockSpec((1, d, s), lambda i: (i, 0, 0)),
            pl.BlockSpec((d, 1), lambda i: (0, 0)),
            pl.BlockSpec((3 * d, d), lambda i: (0, 0)),
        ],
        out_specs=pl.BlockSpec((1, d, s), lambda i: (i, 0, 0)),
        out_shape=jax.ShapeDtypeStruct((b, d, s), _F32),
        compiler_params=pltpu.CompilerParams(
            dimension_semantics=("arbitrary",),
            vmem_limit_bytes=V7X_VMEM_LIMIT_BYTES),
        name="conv_z_meta",
    )(hT, g_col, winT)


def _conv_ffn_call(hT, halo, g1_col, winT, cw, woutT, g2_col, wguT, wdT, tm):
    b, d, s = hT.shape
    const = lambda shape: pl.BlockSpec(shape, lambda i, t: (0,) * len(shape),
                                       pipeline_mode=pl.Buffered(1))
    return pl.pallas_call(
        functools.partial(_conv_ffn_kernel, tm=tm),
        grid=(b, s // tm),
        in_specs=[
            pl.BlockSpec((1, d, tm), lambda i, t: (i, 0, t)),
            const((d, V7X_LANES)), const((d, 1)), const((3 * d, d)),
            const((CONV_WIDTH, d, 1)), const((d, d)), const((d, 1)),
            const((2 * D_FF, d)), const((d, D_FF)),
        ],
        out_specs=pl.BlockSpec((1, tm, d), lambda i, t: (i, t, 0)),
        out_shape=jax.ShapeDtypeStruct((b, s, d), _F32),
        scratch_shapes=[pltpu.VMEM((d, V7X_LANES), _F32)],
        compiler_params=pltpu.CompilerParams(
            dimension_semantics=("arbitrary", "arbitrary"),
            vmem_limit_bytes=V7X_VMEM_LIMIT_BYTES),
        name="conv_ffn",
    )(hT, halo, g1_col, winT, cw, woutT, g2_col, wguT, wdT)


def _rope_tables(gain, scale, pos0, n):
    half = HEAD_DIM // 2
    inv = 1.0 / (ROPE_THETA ** (jnp.arange(0, HEAD_DIM, 2, dtype=_F32) / HEAD_DIM))
    pos = jnp.arange(pos0, pos0 + n, dtype=_F32)
    ang = pos[None, :] * inv[:, None]
    cos, sin = jnp.cos(ang), jnp.sin(ang)
    g1 = gain[:half, None].astype(_F32) * scale
    g2 = gain[half:, None].astype(_F32) * scale
    return jnp.stack([g1 * cos, g2 * sin, g2 * cos, g1 * sin])


def _token_tile(s):
    for tm in (512, 256, 128):
        if s % tm == 0:
            return tm
    raise ValueError(f"sequence length {s} must be a multiple of 128")


def kernel(x, meta_tokens, mixer_norm_g, ffn_norm_g, attn_w_qkv, attn_q_gain, attn_k_gain, attn_lambda_q1, attn_lambda_k1, attn_lambda_q2, attn_lambda_k2, attn_sub_gain, attn_w_o, conv_w_in, conv_w, conv_w_out, ffn_w_gate_up, ffn_w_down):
    b, s, d = x.shape
    assert d == D_MODEL and meta_tokens.shape == (N_META, D_MODEL)
    assert mixer_norm_g.shape[0] == 2, "layer 0 = differential attention, layer 1 = short conv"
    tm = _token_tile(s)
    tq = 256 if s % 256 == 0 else 128

    wqkvT = attn_w_qkv[0].T.astype(_BF16)
    woT = attn_w_o[0].T.astype(_BF16)
    winT = conv_w_in[0].T.astype(_BF16)
    woutT = conv_w_out[0].T.astype(_BF16)
    wguT = [ffn_w_gate_up[i].T.astype(_BF16) for i in range(2)]
    wdT = [ffn_w_down[i].T.astype(_BF16) for i in range(2)]
    cw = conv_w[0].reshape(CONV_WIDTH, D_MODEL, 1).astype(_F32)
    g_mix0_row = mixer_norm_g[0].reshape(1, D_MODEL)
    g_mix1_col = mixer_norm_g[1].reshape(D_MODEL, 1)
    g_ffn_col = [ffn_norm_g[i].reshape(D_MODEL, 1) for i in range(2)]
    gsub_col = attn_sub_gain[0].reshape(V_DIM, 1)
    lam_params = [p[0].reshape(1, HEAD_DIM) for p in
                  (attn_lambda_q1, attn_lambda_k1, attn_lambda_q2, attn_lambda_k2)]
    q_scale = HEAD_DIM ** -0.5 * math.log2(math.e)

    x_meta = jnp.zeros((1, META_PAD, D_MODEL), x.dtype).at[0, :N_META].set(meta_tokens.astype(x.dtype))
    tabq_m = _rope_tables(attn_q_gain[0], q_scale, 0, META_PAD)
    tabk_m = _rope_tables(attn_k_gain[0], 1.0, 0, META_PAD)
    qT_m, k_m, vT_m = _qkv_call(x_meta, g_mix0_row, wqkvT, tabq_m, tabk_m, META_PAD)
    oT_m = _attn_call(qT_m, k_m, vT_m, k_m[0], vT_m[0], lam_params, gsub_col,
                      tq=META_PAD, pos_off=0, has_real=False)
    h1T_m = _attn_out_ffn_call(x_meta, oT_m, woT, g_ffn_col[0], wguT[0], wdT[0], META_PAD)
    zT_m = _conv_z_call(h1T_m, g_mix1_col, winT)
    halo = jnp.zeros((D_MODEL, V7X_LANES), _F32).at[:, V7X_LANES - 2:].set(
        zT_m[0][:, N_META - 2:N_META])

    tabq = _rope_tables(attn_q_gain[0], q_scale, N_META, s)
    tabk = _rope_tables(attn_k_gain[0], 1.0, N_META, s)
    qT, k, vT = _qkv_call(x, g_mix0_row, wqkvT, tabq, tabk, tm)
    shift = (HEAD_DIM * q_scale * jnp.max(jnp.abs(attn_q_gain[0]))
             * jnp.max(jnp.abs(attn_k_gain[0]))).astype(_F32).reshape(1, 1)
    oT = lax.cond(
        shift[0, 0] <= FAST_SOFTMAX_MAX_SHIFT,
        lambda: _attn_fast_call(qT, k, vT, k_m[0], vT_m[0], lam_params, gsub_col, shift, tq=tq),
        lambda: _attn_call(qT, k, vT, k_m[0], vT_m[0], lam_params, gsub_col,
                           tq=tq, pos_off=N_META, has_real=True))
    h1T = _attn_out_ffn_call(x, oT, woT, g_ffn_col[0], wguT[0], wdT[0], tm)
    return _conv_ffn_call(h1T, halo, g_mix1_col, winT, cw, woutT, g_ffn_col[1], wguT[1], wdT[1], tm)
```

```python
import functools
import math

import jax
import jax.numpy as jnp
from jax import lax
from jax.experimental import pallas as pl
from jax.experimental.pallas import tpu as pltpu

D_MODEL = 1024
N_META = 16
N_HEADS = 8
HEAD_DIM = 64
V_DIM = 2 * HEAD_DIM
D_FF = 2816
ROPE_THETA = 10000.0
CONV_WIDTH = 3
EPS = 1e-6
LAMBDA_INIT_L0 = 0.8 - 0.6

V7X_LANES = 128
META_PAD = V7X_LANES
V7X_VMEM_LIMIT_BYTES = 56 * 1024 * 1024
N_SCORE_BUFS = 3
FAST_ATTN_IN_FLIGHT = 2
FAST_SOFTMAX_MAX_SHIFT = 50.0
MASKED = -1e30
FF_CHUNKS = ((0, 1024), (1024, 2048), (2048, D_FF))

_F32 = jnp.float32
_BF16 = jnp.bfloat16


def _dot(a, b):
    return jnp.dot(a, b, preferred_element_type=_F32)


def _prenorm_rows(hT, g_col):
    r = lax.rsqrt(jnp.mean(hT * hT, axis=0, keepdims=True) + EPS)
    return (hT * g_col).astype(_BF16), r


def _rmsnorm_rows(hT, g_col):
    ms = jnp.mean(hT * hT, axis=0, keepdims=True)
    return hT * lax.rsqrt(ms + EPS) * g_col


def _qkv_kernel(x_ref, g_ref, wT_ref, tabq_ref, tabk_ref, qT_ref, k_ref, vT_ref):
    x = x_ref[0]
    ms = jnp.mean(x * x, axis=-1, keepdims=True)
    hn = (x * lax.rsqrt(ms + EPS) * g_ref[...]).astype(_BF16)
    def proj(sec):
        return lax.dot_general(wT_ref[sec * D_MODEL:(sec + 1) * D_MODEL, :], hn,
                               (((1,), (1,)), ((), ())), preferred_element_type=_F32)

    qf, kf, vf = proj(0), proj(1), proj(2)
    tabq = tabq_ref[...]
    tabk = tabk_ref[...]
    half = HEAD_DIM // 2

    def norm_rope(y, tab):
        yn = y * lax.rsqrt(jnp.mean(y * y, axis=0, keepdims=True) + EPS)
        x1, x2 = yn[:half], yn[half:]
        return jnp.concatenate([x1 * tab[0] - x2 * tab[1], x2 * tab[2] + x1 * tab[3]], axis=0)

    for i in range(2 * N_HEADS):
        r = i * HEAD_DIM
        qT_ref[0, r:r + HEAD_DIM, :] = norm_rope(qf[r:r + HEAD_DIM], tabq).astype(_BF16)
    for h in range(N_HEADS):
        r = h * V_DIM
        kh = jnp.concatenate([norm_rope(kf[r:r + HEAD_DIM], tabk),
                              norm_rope(kf[r + HEAD_DIM:r + V_DIM], tabk)], axis=0)
        k_ref[0, :, h * V_DIM:(h + 1) * V_DIM] = kh.T.astype(_BF16)
    vT_ref[0] = vf.astype(_BF16)


def _qkv_call(x, g_row, wT, tabq, tabk, tm):
    b, s, d = x.shape
    grid = (b, s // tm)
    const = lambda shape: pl.BlockSpec(shape, lambda i, t: (0,) * len(shape),
                                       pipeline_mode=pl.Buffered(1))
    return pl.pallas_call(
        _qkv_kernel,
        grid=grid,
        in_specs=[
            pl.BlockSpec((1, tm, d), lambda i, t: (i, t, 0)),
            const((1, d)),
            const((3 * d, d)),
            pl.BlockSpec((4, HEAD_DIM // 2, tm), lambda i, t: (0, 0, t)),
            pl.BlockSpec((4, HEAD_DIM // 2, tm), lambda i, t: (0, 0, t)),
        ],
        out_specs=[
            pl.BlockSpec((1, d, tm), lambda i, t: (i, 0, t)),
            pl.BlockSpec((1, tm, d), lambda i, t: (i, t, 0)),
            pl.BlockSpec((1, d, tm), lambda i, t: (i, 0, t)),
        ],
        out_shape=[
            jax.ShapeDtypeStruct((b, d, s), _BF16),
            jax.ShapeDtypeStruct((b, s, d), _BF16),
            jax.ShapeDtypeStruct((b, d, s), _BF16),
        ],
        compiler_params=pltpu.CompilerParams(
            dimension_semantics=("arbitrary", "arbitrary"),
            vmem_limit_bytes=V7X_VMEM_LIMIT_BYTES),
        name="qkv_proj",
    )(x, g_row, wT, tabq, tabk)


def _attn_kernel(qT_ref, k_ref, vT_ref, km_ref, vmT_ref, lq1_ref, lk1_ref, lq2_ref, lk2_ref,
                 gsub_ref, oT_ref, s_ref, p_ref, *, n_qblk, tq, pos_off, has_real):
    lam = (jnp.exp(jnp.sum(lq1_ref[...] * lk1_ref[...], keepdims=True))
           - jnp.exp(jnp.sum(lq2_ref[...] * lk2_ref[...], keepdims=True))
           + LAMBDA_INIT_L0)
    zero_half = jnp.zeros((HEAD_DIM, tq), _BF16)
    key_row = lax.broadcasted_iota(jnp.int32, (tq, 2 * tq), 0)
    qry_col = lax.broadcasted_iota(jnp.int32, (tq, 2 * tq), 1) % tq
    diag_mask = key_row <= qry_col
    meta_row = lax.broadcasted_iota(jnp.int32, (N_META, 2 * tq), 0)
    meta_col = lax.broadcasted_iota(jnp.int32, (N_META, 2 * tq), 1) % tq
    meta_zero_rows = jnp.zeros((META_PAD - N_META, tq), _BF16)

    def col_reduce(fn, v):
        return fn(v.reshape(v.shape[0] // 8, 8, v.shape[1]), axis=0)

    def query_block(qi, slot):
        qT = qT_ref[0, :, qi * tq:(qi + 1) * tq]
        qab = jnp.concatenate([jnp.concatenate([qT[:HEAD_DIM], zero_half], axis=0),
                               jnp.concatenate([zero_half, qT[HEAD_DIM:]], axis=0)], axis=1)
        q0 = pos_off + qi * tq
        meta_mask = None if q0 >= N_META - 1 else meta_row <= meta_col + q0
        chunks = [(0, N_META, km_ref[0:N_META, :], vmT_ref[...], meta_mask)]
        if has_real:
            for j in range(qi + 1):
                chunks.append((N_META + j * tq, tq, k_ref[0, j * tq:(j + 1) * tq, :],
                               vT_ref[0, :, j * tq:(j + 1) * tq], diag_mask if j == qi else None))
        s_buf = s_ref.at[slot]
        p_buf = p_ref.at[slot]
        m8 = jnp.full((8, 2 * tq), MASKED, _F32)
        for r0, nr, kc, _, mask in chunks:
            s = _dot(kc, qab)
            if mask is not None:
                s = jnp.where(mask, s, MASKED)
            s_buf[r0:r0 + nr, :] = s
            m8 = jnp.maximum(m8, col_reduce(jnp.max, s))
            yield False
        m = jnp.max(m8, axis=0, keepdims=True)
        yield True
        l8 = jnp.zeros((8, 2 * tq), _F32)
        for r0, nr, _, _, _ in chunks:
            p = jnp.exp2(s_buf[r0:r0 + nr, :] - m)
            l8 = l8 + col_reduce(jnp.sum, p)
            p_buf[r0:r0 + nr, :] = p.astype(_BF16)
            yield False
        l = jnp.sum(l8, axis=0, keepdims=True)
        l1, l2 = l[:, :tq], l[:, tq:]
        rho = (lam * l1 / l2).astype(_BF16)
        yield True
        acc = jnp.zeros((V_DIM, tq), _F32)
        for r0, nr, _, vc, _ in chunks:
            pb = p_buf[r0:r0 + nr, :]
            a = pb[:, :tq] - rho * pb[:, tq:]
            if nr == N_META:
                a = jnp.concatenate([a, meta_zero_rows], axis=0)
            acc = acc + _dot(vc, a)
            yield False
        on = _rmsnorm_rows(acc * (1.0 / l1), gsub_ref[...]) * (1.0 - LAMBDA_INIT_L0)
        oT_ref[0, :, qi * tq:(qi + 1) * tq] = on.astype(_BF16)
        yield True

    order = list(range(1, n_qblk, 2)) + list(range(n_qblk - 1 - (n_qblk - 1) % 2, -1, -2))
    active = []
    for stage in range(n_qblk + 2):
        if stage < n_qblk:
            active.append(query_block(order[stage], stage % N_SCORE_BUFS))
        running = list(active)
        while running:
            running = [g for g in running if not next(g)]
        if stage >= 2:
            active.pop(0)


def _attn_fast_kernel(qT_ref, k_ref, vT_ref, km_ref, vmT_ref, lq1_ref, lk1_ref, lq2_ref, lk2_ref,
                      gsub_ref, shift_ref, oT_ref, *, n_qblk, tq):
    lam = (jnp.exp(jnp.sum(lq1_ref[...] * lk1_ref[...], keepdims=True))
           - jnp.exp(jnp.sum(lq2_ref[...] * lk2_ref[...], keepdims=True))
           + LAMBDA_INIT_L0)
    shift = shift_ref[...]
    zero_half = jnp.zeros((HEAD_DIM, tq), _BF16)
    key_row = lax.broadcasted_iota(jnp.int32, (tq, 2 * tq), 0)
    qry_col = lax.broadcasted_iota(jnp.int32, (tq, 2 * tq), 1) % tq
    diag_mask = key_row <= qry_col
    meta_zero_rows = jnp.zeros((META_PAD - N_META, tq), _BF16)

    def col_sum(v):
        return jnp.sum(v.reshape(v.shape[0] // 8, 8, v.shape[1]), axis=0)

    def query_block(qi):
        qT = qT_ref[0, :, qi * tq:(qi + 1) * tq]
        qab = jnp.concatenate([jnp.concatenate([qT[:HEAD_DIM], zero_half], axis=0),
                               jnp.concatenate([zero_half, qT[HEAD_DIM:]], axis=0)], axis=1)
        chunks = [(km_ref[0:N_META, :], vmT_ref[...], None)]
        for j in range(qi + 1):
            chunks.append((k_ref[0, j * tq:(j + 1) * tq, :], vT_ref[0, :, j * tq:(j + 1) * tq],
                           diag_mask if j == qi else None))
        l8 = jnp.zeros((8, 2 * tq), _F32)
        acc1 = jnp.zeros((V_DIM, tq), _F32)
        acc2 = jnp.zeros((V_DIM, tq), _F32)
        s_next = _dot(chunks[0][0], qab)
        yield
        for ci, (kc, vc, mask) in enumerate(chunks):
            s = s_next
            if ci + 1 < len(chunks):
                s_next = _dot(chunks[ci + 1][0], qab)
                yield
            if mask is not None:
                s = jnp.where(mask, s, MASKED)
            p = jnp.exp2(s - shift)
            l8 = l8 + col_sum(p)
            pb = p.astype(_BF16)
            p1, p2 = pb[:, :tq], pb[:, tq:]
            if kc.shape[0] == N_META:
                p1 = jnp.concatenate([p1, meta_zero_rows], axis=0)
                p2 = jnp.concatenate([p2, meta_zero_rows], axis=0)
            acc1 = acc1 + _dot(vc, p1)
            acc2 = acc2 + _dot(vc, p2)
            yield
        l = jnp.sum(l8, axis=0, keepdims=True)
        o = acc1 * (1.0 / l[:, :tq]) - acc2 * (lam / l[:, tq:])
        on = _rmsnorm_rows(o, gsub_ref[...]) * (1.0 - LAMBDA_INIT_L0)
        oT_ref[0, :, qi * tq:(qi + 1) * tq] = on.astype(_BF16)

    pending = [query_block(qi) for qi in range(n_qblk - 1, -1, -1)]
    active = []
    while pending or active:
        while pending and len(active) < FAST_ATTN_IN_FLIGHT:
            active.append(pending.pop(0))
        for g in list(active):
            try:
                next(g)
            except StopIteration:
                active.remove(g)


def _attn_fast_call(qT, k, vT, k_meta, vT_meta, lam_params, gsub_col, shift, *, tq):
    b, d, s = qT.shape
    kernel = functools.partial(_attn_fast_kernel, n_qblk=s // tq, tq=tq)
    lam_spec = pl.BlockSpec((1, HEAD_DIM), lambda i, h: (0, 0))
    return pl.pallas_call(
        kernel,
        grid=(b, N_HEADS),
        in_specs=[
            pl.BlockSpec((1, V_DIM, s), lambda i, h: (i, h, 0)),
            pl.BlockSpec((1, s, V_DIM), lambda i, h: (i, 0, h)),
            pl.BlockSpec((1, V_DIM, s), lambda i, h: (i, h, 0)),
            pl.BlockSpec((META_PAD, V_DIM), lambda i, h: (0, h)),
            pl.BlockSpec((V_DIM, META_PAD), lambda i, h: (h, 0)),
            lam_spec, lam_spec, lam_spec, lam_spec,
            pl.BlockSpec((V_DIM, 1), lambda i, h: (0, 0)),
            pl.BlockSpec((1, 1), lambda i, h: (0, 0)),
        ],
        out_specs=pl.BlockSpec((1, V_DIM, s), lambda i, h: (i, h, 0)),
        out_shape=jax.ShapeDtypeStruct((b, d, s), _BF16),
        compiler_params=pltpu.CompilerParams(
            dimension_semantics=("arbitrary", "arbitrary"),
            vmem_limit_bytes=V7X_VMEM_LIMIT_BYTES),
        name="diff_attn_fast",
    )(qT, k, vT, k_meta, vT_meta, *lam_params, gsub_col, shift)


def _attn_call(qT, k, vT, k_meta, vT_meta, lam_params, gsub_col, *, tq, pos_off, has_real):
    b, d, s = qT.shape
    kernel = functools.partial(_attn_kernel, n_qblk=s // tq, tq=tq, pos_off=pos_off,
                               has_real=has_real)
    lam_spec = pl.BlockSpec((1, HEAD_DIM), lambda i, h: (0, 0))
    return pl.pallas_call(
        kernel,
        grid=(b, N_HEADS),
        in_specs=[
            pl.BlockSpec((1, V_DIM, s), lambda i, h: (i, h, 0)),
            pl.BlockSpec((1, s, V_DIM), lambda i, h: (i, 0, h)),
            pl.BlockSpec((1, V_DIM, s), lambda i, h: (i, h, 0)),
            pl.BlockSpec((META_PAD, V_DIM), lambda i, h: (0, h)),
            pl.BlockSpec((V_DIM, META_PAD), lambda i, h: (h, 0)),
            lam_spec, lam_spec, lam_spec, lam_spec,
            pl.BlockSpec((V_DIM, 1), lambda i, h: (0, 0)),
        ],
        out_specs=pl.BlockSpec((1, V_DIM, s), lambda i, h: (i, h, 0)),
        out_shape=jax.ShapeDtypeStruct((b, d, s), _BF16),
        scratch_shapes=[pltpu.VMEM((N_SCORE_BUFS, N_META + (s if has_real else 0), 2 * tq), _F32),
                        pltpu.VMEM((N_SCORE_BUFS, N_META + (s if has_real else 0), 2 * tq), _BF16)],
        compiler_params=pltpu.CompilerParams(
            dimension_semantics=("arbitrary", "arbitrary"),
            vmem_limit_bytes=V7X_VMEM_LIMIT_BYTES),
        name="diff_attn",
    )(qT, k, vT, k_meta, vT_meta, *lam_params, gsub_col)


def _ffn_T(hT, g_col, wguT_ref, wdT_ref):
    hg, r = _prenorm_rows(hT, g_col)
    acc = None
    for c0, c1 in FF_CHUNKS:
        g = _dot(wguT_ref[c0:c1, :], hg) * r
        u = _dot(wguT_ref[D_FF + c0:D_FF + c1, :], hg) * r
        a = (g * jax.nn.sigmoid(g) * u).astype(_BF16)
        part = _dot(wdT_ref[:, c0:c1], a)
        acc = part if acc is None else acc + part
    return acc


def _attn_out_ffn_kernel(x_ref, oT_ref, woT_ref, g_ref, wguT_ref, wdT_ref, hT_ref):
    h = x_ref[0].T + _dot(woT_ref[...], oT_ref[0])
    hT_ref[0] = h + _ffn_T(h, g_ref[...], wguT_ref, wdT_ref)


def _attn_out_ffn_call(x, oT, woT, g_col, wguT, wdT, tm):
    b, s, d = x.shape
    const = lambda shape: pl.BlockSpec(shape, lambda i, t: (0,) * len(shape),
                                       pipeline_mode=pl.Buffered(1))
    return pl.pallas_call(
        _attn_out_ffn_kernel,
        grid=(b, s // tm),
        in_specs=[
            pl.BlockSpec((1, tm, d), lambda i, t: (i, t, 0)),
            pl.BlockSpec((1, d, tm), lambda i, t: (i, 0, t)),
            const((d, d)), const((d, 1)), const((2 * D_FF, d)), const((d, D_FF)),
        ],
        out_specs=pl.BlockSpec((1, d, tm), lambda i, t: (i, 0, t)),
        out_shape=jax.ShapeDtypeStruct((b, d, s), _F32),
        compiler_params=pltpu.CompilerParams(
            dimension_semantics=("arbitrary", "arbitrary"),
            vmem_limit_bytes=V7X_VMEM_LIMIT_BYTES),
        name="attn_out_ffn",
    )(x, oT, woT, g_col, wguT, wdT)


def _conv_gate_T(hT, g_col, winT_ref):
    hg, r = _prenorm_rows(hT, g_col)
    bcu = _dot(winT_ref[...], hg)
    return bcu[:D_MODEL] * r, bcu[D_MODEL:2 * D_MODEL] * bcu[2 * D_MODEL:] * (r * r)


def _conv_z_kernel(hT_ref, g_ref, winT_ref, zT_ref):
    _, z = _conv_gate_T(hT_ref[0], g_ref[...], winT_ref)
    zT_ref[0] = z


def _conv_ffn_kernel(hT_ref, halo_ref, g1_ref, winT_ref, cw_ref, woutT_ref, g2_ref, wguT_ref,
                     wdT_ref, out_ref, zprev_ref, *, tm):
    @pl.when(pl.program_id(1) == 0)
    def _():
        zprev_ref[...] = halo_ref[...]

    h = hT_ref[0]
    gate_b, z = _conv_gate_T(h, g1_ref[...], winT_ref)
    zext = jnp.concatenate([zprev_ref[...], z], axis=1)
    z1 = pltpu.roll(zext, 1, 1)[:, V7X_LANES:]
    z2 = pltpu.roll(zext, 2, 1)[:, V7X_LANES:]
    zc = cw_ref[2] * z + cw_ref[1] * z1 + cw_ref[0] * z2
    zprev_ref[...] = z[:, tm - V7X_LANES:]
    h = h + _dot(woutT_ref[...], (gate_b * zc).astype(_BF16))
    h = h + _ffn_T(h, g2_ref[...], wguT_ref, wdT_ref)
    out_ref[0] = h.T


def _conv_z_call(hT, g_col, winT):
    b, d, s = hT.shape
    return pl.pallas_call(
        _conv_z_kernel,
        grid=(b,),
        in_specs=[
            pl.BlockSpec((1, d, s), lambda i: (i, 0, 0)),
            pl.BlockSpec((d, 1), lambda i: (0, 0)),
            pl.BlockSpec((3 * d, d), lambda i: (0, 0)),
        ],
        out_specs=pl.BlockSpec((1, d, s), lambda i: (i, 0, 0)),
        out_shape=jax.ShapeDtypeStruct((b, d, s), _F32),
        compiler_params=pltpu.CompilerParams(
            dimension_semantics=("arbitrary",),
            vmem_limit_bytes=V7X_VMEM_LIMIT_BYTES),
        name="conv_z_meta",
    )(hT, g_col, winT)


def _conv_ffn_call(hT, halo, g1_col, winT, cw, woutT, g2_col, wguT, wdT, tm):
    b, d, s = hT.shape
    const = lambda shape: pl.BlockSpec(shape, lambda i, t: (0,) * len(shape),
                                       pipeline_mode=pl.Buffered(1))
    return pl.pallas_call(
        functools.partial(_conv_ffn_kernel, tm=tm),
        grid=(b, s // tm),
        in_specs=[
            pl.BlockSpec((1, d, tm), lambda i, t: (i, 0, t)),
            const((d, V7X_LANES)), const((d, 1)), const((3 * d, d)),
            const((CONV_WIDTH, d, 1)), const((d, d)), const((d, 1)),
            const((2 * D_FF, d)), const((d, D_FF)),
        ],
        out_specs=pl.BlockSpec((1, tm, d), lambda i, t: (i, t, 0)),
        out_shape=jax.ShapeDtypeStruct((b, s, d), _F32),
        scratch_shapes=[pltpu.VMEM((d, V7X_LANES), _F32)],
        compiler_params=pltpu.CompilerParams(
            dimension_semantics=("arbitrary", "arbitrary"),
            vmem_limit_bytes=V7X_VMEM_LIMIT_BYTES),
        name="conv_ffn",
    )(hT, halo, g1_col, winT, cw, woutT, g2_col, wguT, wdT)


def _rope_tables(gain, scale, pos0, n):
    half = HEAD_DIM // 2
    inv = 1.0 / (ROPE_THETA ** (jnp.arange(0, HEAD_DIM, 2, dtype=_F32) / HEAD_DIM))
    pos = jnp.arange(pos0, pos0 + n, dtype=_F32)
    ang = pos[None, :] * inv[:, None]
    cos, sin = jnp.cos(ang), jnp.sin(ang)
    g1 = gain[:half, None].astype(_F32) * scale
    g2 = gain[half:, None].astype(_F32) * scale
    return jnp.stack([g1 * cos, g2 * sin, g2 * cos, g1 * sin])


def _token_tile(s):
    for tm in (512, 256, 128):
        if s % tm == 0:
            return tm
    raise ValueError(f"sequence length {s} must be a multiple of 128")


def kernel(x, meta_tokens, mixer_norm_g, ffn_norm_g, attn_w_qkv, attn_q_gain, attn_k_gain, attn_lambda_q1, attn_lambda_k1, attn_lambda_q2, attn_lambda_k2, attn_sub_gain, attn_w_o, conv_w_in, conv_w, conv_w_out, ffn_w_gate_up, ffn_w_down):
    b, s, d = x.shape
    assert d == D_MODEL and meta_tokens.shape == (N_META, D_MODEL)
    assert mixer_norm_g.shape[0] == 2, "layer 0 = differential attention, layer 1 = short conv"
    tm = _token_tile(s)
    tq = 256 if s % 256 == 0 else 128

    wqkvT = attn_w_qkv[0].T.astype(_BF16)
    woT = attn_w_o[0].T.astype(_BF16)
    winT = conv_w_in[0].T.astype(_BF16)
    woutT = conv_w_out[0].T.astype(_BF16)
    wguT = [ffn_w_gate_up[i].T.astype(_BF16) for i in range(2)]
    wdT = [ffn_w_down[i].T.astype(_BF16) for i in range(2)]
    cw = conv_w[0].reshape(CONV_WIDTH, D_MODEL, 1).astype(_F32)
    g_mix0_row = mixer_norm_g[0].reshape(1, D_MODEL)
    g_mix1_col = mixer_norm_g[1].reshape(D_MODEL, 1)
    g_ffn_col = [ffn_norm_g[i].reshape(D_MODEL, 1) for i in range(2)]
    gsub_col = attn_sub_gain[0].reshape(V_DIM, 1)
    lam_params = [p[0].reshape(1, HEAD_DIM) for p in
                  (attn_lambda_q1, attn_lambda_k1, attn_lambda_q2, attn_lambda_k2)]
    q_scale = HEAD_DIM ** -0.5 * math.log2(math.e)

    x_meta = jnp.zeros((1, META_PAD, D_MODEL), x.dtype).at[0, :N_META].set(meta_tokens.astype(x.dtype))
    tabq_m = _rope_tables(attn_q_gain[0], q_scale, 0, META_PAD)
    tabk_m = _rope_tables(attn_k_gain[0], 1.0, 0, META_PAD)
    qT_m, k_m, vT_m = _qkv_call(x_meta, g_mix0_row, wqkvT, tabq_m, tabk_m, META_PAD)
    oT_m = _attn_call(qT_m, k_m, vT_m, k_m[0], vT_m[0], lam_params, gsub_col,
                      tq=META_PAD, pos_off=0, has_real=False)
    h1T_m = _attn_out_ffn_call(x_meta, oT_m, woT, g_ffn_col[0], wguT[0], wdT[0], META_PAD)
    zT_m = _conv_z_call(h1T_m, g_mix1_col, winT)
    halo = jnp.zeros((D_MODEL, V7X_LANES), _F32).at[:, V7X_LANES - 2:].set(
        zT_m[0][:, N_META - 2:N_META])

    tabq = _rope_tables(attn_q_gain[0], q_scale, N_META, s)
    tabk = _rope_tables(attn_k_gain[0], 1.0, N_META, s)
    qT, k, vT = _qkv_call(x, g_mix0_row, wqkvT, tabq, tabk, tm)
    shift = (HEAD_DIM * q_scale * jnp.max(jnp.abs(attn_q_gain[0]))
             * jnp.max(jnp.abs(attn_k_gain[0]))).astype(_F32).reshape(1, 1)
    oT = lax.cond(
        shift[0, 0] <= FAST_SOFTMAX_MAX_SHIFT,
        lambda: _attn_fast_call(qT, k, vT, k_m[0], vT_m[0], lam_params, gsub_col, shift, tq=tq),
        lambda: _attn_call(qT, k, vT, k_m[0], vT_m[0], lam_params, gsub_col,
                           tq=tq, pos_off=N_META, has_real=True))
    h1T = _attn_out_ffn_call(x, oT, woT, g_ffn_col[0], wguT[0], wdT[0], tm)
    return _conv_ffn_call(h1T, halo, g_mix1_col, winT, cw, woutT, g_ffn_col[1], wguT[1], wdT[1], tm)
```

```python
import functools
import math

import jax
import jax.numpy as jnp
from jax import lax
from jax.experimental import pallas as pl
from jax.experimental.pallas import tpu as pltpu

D_MODEL = 1024
N_META = 16
N_HEADS = 8
HEAD_DIM = 64
V_DIM = 2 * HEAD_DIM
D_FF = 2816
ROPE_THETA = 10000.0
CONV_WIDTH = 3
EPS = 1e-6
LAMBDA_INIT_L0 = 0.8 - 0.6

V7X_LANES = 128
META_PAD = V7X_LANES
V7X_VMEM_LIMIT_BYTES = 56 * 1024 * 1024
N_SCORE_BUFS = 3
FAST_ATTN_HEADS_PER_STEP = 2
FAST_ATTN_IN_FLIGHT = 2
FAST_SOFTMAX_MAX_SHIFT = 50.0
MASKED = -1e30
FF_CHUNKS = ((0, 1024), (1024, 2048), (2048, D_FF))

_F32 = jnp.float32
_BF16 = jnp.bfloat16


def _dot(a, b):
    return jnp.dot(a, b, preferred_element_type=_F32)


def _prenorm_rows(hT, g_col):
    r = lax.rsqrt(jnp.mean(hT * hT, axis=0, keepdims=True) + EPS)
    return (hT * g_col).astype(_BF16), r


def _rmsnorm_rows(hT, g_col):
    ms = jnp.mean(hT * hT, axis=0, keepdims=True)
    return hT * lax.rsqrt(ms + EPS) * g_col


def _qkv_kernel(x_ref, g_ref, wT_ref, tabq_ref, tabk_ref, qT_ref, k_ref, vT_ref):
    x = x_ref[0]
    ms = jnp.mean(x * x, axis=-1, keepdims=True)
    hn = (x * lax.rsqrt(ms + EPS) * g_ref[...]).astype(_BF16)
    def proj(sec):
        return lax.dot_general(wT_ref[sec * D_MODEL:(sec + 1) * D_MODEL, :], hn,
                               (((1,), (1,)), ((), ())), preferred_element_type=_F32)

    qf, kf, vf = proj(0), proj(1), proj(2)
    tabq = tabq_ref[...]
    tabk = tabk_ref[...]
    half = HEAD_DIM // 2

    def norm_rope(y, tab):
        yn = y * lax.rsqrt(jnp.mean(y * y, axis=0, keepdims=True) + EPS)
        x1, x2 = yn[:half], yn[half:]
        return jnp.concatenate([x1 * tab[0] - x2 * tab[1], x2 * tab[2] + x1 * tab[3]], axis=0)

    for i in range(2 * N_HEADS):
        r = i * HEAD_DIM
        qT_ref[0, r:r + HEAD_DIM, :] = norm_rope(qf[r:r + HEAD_DIM], tabq).astype(_BF16)
    for h in range(N_HEADS):
        r = h * V_DIM
        kh = jnp.concatenate([norm_rope(kf[r:r + HEAD_DIM], tabk),
                              norm_rope(kf[r + HEAD_DIM:r + V_DIM], tabk)], axis=0)
        k_ref[0, :, h * V_DIM:(h + 1) * V_DIM] = kh.T.astype(_BF16)
    vT_ref[0] = vf.astype(_BF16)


def _qkv_call(x, g_row, wT, tabq, tabk, tm):
    b, s, d = x.shape
    grid = (b, s // tm)
    const = lambda shape: pl.BlockSpec(shape, lambda i, t: (0,) * len(shape),
                                       pipeline_mode=pl.Buffered(1))
    return pl.pallas_call(
        _qkv_kernel,
        grid=grid,
        in_specs=[
            pl.BlockSpec((1, tm, d), lambda i, t: (i, t, 0)),
            const((1, d)),
            const((3 * d, d)),
            pl.BlockSpec((4, HEAD_DIM // 2, tm), lambda i, t: (0, 0, t)),
            pl.BlockSpec((4, HEAD_DIM // 2, tm), lambda i, t: (0, 0, t)),
        ],
        out_specs=[
            pl.BlockSpec((1, d, tm), lambda i, t: (i, 0, t)),
            pl.BlockSpec((1, tm, d), lambda i, t: (i, t, 0)),
            pl.BlockSpec((1, d, tm), lambda i, t: (i, 0, t)),
        ],
        out_shape=[
            jax.ShapeDtypeStruct((b, d, s), _BF16),
            jax.ShapeDtypeStruct((b, s, d), _BF16),
            jax.ShapeDtypeStruct((b, d, s), _BF16),
        ],
        compiler_params=pltpu.CompilerParams(
            dimension_semantics=("arbitrary", "arbitrary"),
            vmem_limit_bytes=V7X_VMEM_LIMIT_BYTES),
        name="qkv_proj",
    )(x, g_row, wT, tabq, tabk)


def _attn_kernel(qT_ref, k_ref, vT_ref, km_ref, vmT_ref, lq1_ref, lk1_ref, lq2_ref, lk2_ref,
                 gsub_ref, oT_ref, s_ref, p_ref, *, n_qblk, tq, pos_off, has_real):
    lam = (jnp.exp(jnp.sum(lq1_ref[...] * lk1_ref[...], keepdims=True))
           - jnp.exp(jnp.sum(lq2_ref[...] * lk2_ref[...], keepdims=True))
           + LAMBDA_INIT_L0)
    zero_half = jnp.zeros((HEAD_DIM, tq), _BF16)
    key_row = lax.broadcasted_iota(jnp.int32, (tq, 2 * tq), 0)
    qry_col = lax.broadcasted_iota(jnp.int32, (tq, 2 * tq), 1) % tq
    diag_mask = key_row <= qry_col
    meta_row = lax.broadcasted_iota(jnp.int32, (N_META, 2 * tq), 0)
    meta_col = lax.broadcasted_iota(jnp.int32, (N_META, 2 * tq), 1) % tq
    meta_zero_rows = jnp.zeros((META_PAD - N_META, tq), _BF16)

    def col_reduce(fn, v):
        return fn(v.reshape(v.shape[0] // 8, 8, v.shape[1]), axis=0)

    def query_block(qi, slot):
        qT = qT_ref[0, :, qi * tq:(qi + 1) * tq]
        qab = jnp.concatenate([jnp.concatenate([qT[:HEAD_DIM], zero_half], axis=0),
                               jnp.concatenate([zero_half, qT[HEAD_DIM:]], axis=0)], axis=1)
        q0 = pos_off + qi * tq
        meta_mask = None if q0 >= N_META - 1 else meta_row <= meta_col + q0
        chunks = [(0, N_META, km_ref[0:N_META, :], vmT_ref[...], meta_mask)]
        if has_real:
            for j in range(qi + 1):
                chunks.append((N_META + j * tq, tq, k_ref[0, j * tq:(j + 1) * tq, :],
                               vT_ref[0, :, j * tq:(j + 1) * tq], diag_mask if j == qi else None))
        s_buf = s_ref.at[slot]
        p_buf = p_ref.at[slot]
        m8 = jnp.full((8, 2 * tq), MASKED, _F32)
        for r0, nr, kc, _, mask in chunks:
            s = _dot(kc, qab)
            if mask is not None:
                s = jnp.where(mask, s, MASKED)
            s_buf[r0:r0 + nr, :] = s
            m8 = jnp.maximum(m8, col_reduce(jnp.max, s))
            yield False
        m = jnp.max(m8, axis=0, keepdims=True)
        yield True
        l8 = jnp.zeros((8, 2 * tq), _F32)
        for r0, nr, _, _, _ in chunks:
            p = jnp.exp2(s_buf[r0:r0 + nr, :] - m)
            l8 = l8 + col_reduce(jnp.sum, p)
            p_buf[r0:r0 + nr, :] = p.astype(_BF16)
            yield False
        l = jnp.sum(l8, axis=0, keepdims=True)
        l1, l2 = l[:, :tq], l[:, tq:]
        rho = (lam * l1 / l2).astype(_BF16)
        yield True
        acc = jnp.zeros((V_DIM, tq), _F32)
        for r0, nr, _, vc, _ in chunks:
            pb = p_buf[r0:r0 + nr, :]
            a = pb[:, :tq] - rho * pb[:, tq:]
            if nr == N_META:
                a = jnp.concatenate([a, meta_zero_rows], axis=0)
            acc = acc + _dot(vc, a)
            yield False
        on = _rmsnorm_rows(acc * (1.0 / l1), gsub_ref[...]) * (1.0 - LAMBDA_INIT_L0)
        oT_ref[0, :, qi * tq:(qi + 1) * tq] = on.astype(_BF16)
        yield True

    order = list(range(1, n_qblk, 2)) + list(range(n_qblk - 1 - (n_qblk - 1) % 2, -1, -2))
    active = []
    for stage in range(n_qblk + 2):
        if stage < n_qblk:
            active.append(query_block(order[stage], stage % N_SCORE_BUFS))
        running = list(active)
        while running:
            running = [g for g in running if not next(g)]
        if stage >= 2:
            active.pop(0)


def _attn_fast_kernel(qT_ref, k_ref, vT_ref, km_ref, vmT_ref, lq1_ref, lk1_ref, lq2_ref, lk2_ref,
                      gsub_ref, shift_ref, oT_ref, *, n_qblk, tq, heads):
    lam = (jnp.exp(jnp.sum(lq1_ref[...] * lk1_ref[...], keepdims=True))
           - jnp.exp(jnp.sum(lq2_ref[...] * lk2_ref[...], keepdims=True))
           + LAMBDA_INIT_L0)
    shift = shift_ref[...]
    zero_half = jnp.zeros((HEAD_DIM, tq), _BF16)
    key_row = lax.broadcasted_iota(jnp.int32, (tq, 2 * tq), 0)
    qry_col = lax.broadcasted_iota(jnp.int32, (tq, 2 * tq), 1) % tq
    diag_mask = key_row <= qry_col
    meta_zero_rows = jnp.zeros((META_PAD - N_META, tq), _BF16)

    def col_sum(v):
        return jnp.sum(v.reshape(v.shape[0] // 8, 8, v.shape[1]), axis=0)

    def query_block(hd, qi):
        h0 = hd * V_DIM
        qT = qT_ref[0, h0:h0 + V_DIM, qi * tq:(qi + 1) * tq]
        qab = jnp.concatenate([jnp.concatenate([qT[:HEAD_DIM], zero_half], axis=0),
                               jnp.concatenate([zero_half, qT[HEAD_DIM:]], axis=0)], axis=1)
        chunks = [(km_ref[0:N_META, h0:h0 + V_DIM], vmT_ref[h0:h0 + V_DIM, :], None)]
        for j in range(qi + 1):
            chunks.append((k_ref[0, j * tq:(j + 1) * tq, h0:h0 + V_DIM],
                           vT_ref[0, h0:h0 + V_DIM, j * tq:(j + 1) * tq],
                           diag_mask if j == qi else None))
        l8 = jnp.zeros((8, 2 * tq), _F32)
        acc1 = jnp.zeros((V_DIM, tq), _F32)
        acc2 = jnp.zeros((V_DIM, tq), _F32)
        s_next = _dot(chunks[0][0], qab)
        yield
        for ci, (kc, vc, mask) in enumerate(chunks):
            s = s_next
            if ci + 1 < len(chunks):
                s_next = _dot(chunks[ci + 1][0], qab)
                yield
            if mask is not None:
                s = jnp.where(mask, s, MASKED)
            p = jnp.exp2(s - shift)
            l8 = l8 + col_sum(p)
            pb = p.astype(_BF16)
            p1, p2 = pb[:, :tq], pb[:, tq:]
            if kc.shape[0] == N_META:
                p1 = jnp.concatenate([p1, meta_zero_rows], axis=0)
                p2 = jnp.concatenate([p2, meta_zero_rows], axis=0)
            acc1 = acc1 + _dot(vc, p1)
            acc2 = acc2 + _dot(vc, p2)
            yield
        l = jnp.sum(l8, axis=0, keepdims=True)
        o = acc1 * (1.0 / l[:, :tq]) - acc2 * (lam / l[:, tq:])
        on = _rmsnorm_rows(o, gsub_ref[...]) * (1.0 - LAMBDA_INIT_L0)
        oT_ref[0, h0:h0 + V_DIM, qi * tq:(qi + 1) * tq] = on.astype(_BF16)

    pending = [query_block(hd, qi) for hd in range(heads) for qi in range(n_qblk - 1, -1, -1)]
    active = []
    while pending or active:
        while pending and len(active) < FAST_ATTN_IN_FLIGHT:
            active.append(pending.pop(0))
        for g in list(active):
            try:
                next(g)
            except StopIteration:
                active.remove(g)


def _attn_fast_call(qT, k, vT, k_meta, vT_meta, lam_params, gsub_col, shift, *, tq):
    b, d, s = qT.shape
    hp = FAST_ATTN_HEADS_PER_STEP
    kernel = functools.partial(_attn_fast_kernel, n_qblk=s // tq, tq=tq, heads=hp)
    lam_spec = pl.BlockSpec((1, HEAD_DIM), lambda i, h: (0, 0))
    return pl.pallas_call(
        kernel,
        grid=(b, N_HEADS // hp),
        in_specs=[
            pl.BlockSpec((1, hp * V_DIM, s), lambda i, h: (i, h, 0)),
            pl.BlockSpec((1, s, hp * V_DIM), lambda i, h: (i, 0, h)),
            pl.BlockSpec((1, hp * V_DIM, s), lambda i, h: (i, h, 0)),
            pl.BlockSpec((META_PAD, hp * V_DIM), lambda i, h: (0, h)),
            pl.BlockSpec((hp * V_DIM, META_PAD), lambda i, h: (h, 0)),
            lam_spec, lam_spec, lam_spec, lam_spec,
            pl.BlockSpec((V_DIM, 1), lambda i, h: (0, 0)),
            pl.BlockSpec((1, 1), lambda i, h: (0, 0)),
        ],
        out_specs=pl.BlockSpec((1, hp * V_DIM, s), lambda i, h: (i, h, 0)),
        out_shape=jax.ShapeDtypeStruct((b, d, s), _BF16),
        compiler_params=pltpu.CompilerParams(
            dimension_semantics=("arbitrary", "arbitrary"),
            vmem_limit_bytes=V7X_VMEM_LIMIT_BYTES),
        name="diff_attn_fast",
    )(qT, k, vT, k_meta, vT_meta, *lam_params, gsub_col, shift)


def _attn_call(qT, k, vT, k_meta, vT_meta, lam_params, gsub_col, *, tq, pos_off, has_real):
    b, d, s = qT.shape
    kernel = functools.partial(_attn_kernel, n_qblk=s // tq, tq=tq, pos_off=pos_off,
                               has_real=has_real)
    lam_spec = pl.BlockSpec((1, HEAD_DIM), lambda i, h: (0, 0))
    return pl.pallas_call(
        kernel,
        grid=(b, N_HEADS),
        in_specs=[
            pl.BlockSpec((1, V_DIM, s), lambda i, h: (i, h, 0)),
            pl.BlockSpec((1, s, V_DIM), lambda i, h: (i, 0, h)),
            pl.BlockSpec((1, V_DIM, s), lambda i, h: (i, h, 0)),
            pl.BlockSpec((META_PAD, V_DIM), lambda i, h: (0, h)),
            pl.BlockSpec((V_DIM, META_PAD), lambda i, h: (h, 0)),
            lam_spec, lam_spec, lam_spec, lam_spec,
            pl.BlockSpec((V_DIM, 1), lambda i, h: (0, 0)),
        ],
        out_specs=pl.BlockSpec((1, V_DIM, s), lambda i, h: (i, h, 0)),
        out_shape=jax.ShapeDtypeStruct((b, d, s), _BF16),
        scratch_shapes=[pltpu.VMEM((N_SCORE_BUFS, N_META + (s if has_real else 0), 2 * tq), _F32),
                        pltpu.VMEM((N_SCORE_BUFS, N_META + (s if has_real else 0), 2 * tq), _BF16)],
        compiler_params=pltpu.CompilerParams(
            dimension_semantics=("arbitrary", "arbitrary"),
            vmem_limit_bytes=V7X_VMEM_LIMIT_BYTES),
        name="diff_attn",
    )(qT, k, vT, k_meta, vT_meta, *lam_params, gsub_col)


def _ffn_T(hT, g_col, wguT_ref, wdT_ref):
    hg, r = _prenorm_rows(hT, g_col)
    acc = None
    for c0, c1 in FF_CHUNKS:
        g = _dot(wguT_ref[c0:c1, :], hg) * r
        u = _dot(wguT_ref[D_FF + c0:D_FF + c1, :], hg) * r
        a = (g * jax.nn.sigmoid(g) * u).astype(_BF16)
        part = _dot(wdT_ref[:, c0:c1], a)
        acc = part if acc is None else acc + part
    return acc


def _attn_out_ffn_kernel(x_ref, oT_ref, woT_ref, g_ref, wguT_ref, wdT_ref, hT_ref):
    h = x_ref[0].T + _dot(woT_ref[...], oT_ref[0])
    hT_ref[0] = h + _ffn_T(h, g_ref[...], wguT_ref, wdT_ref)


def _attn_out_ffn_call(x, oT, woT, g_col, wguT, wdT, tm):
    b, s, d = x.shape
    const = lambda shape: pl.BlockSpec(shape, lambda i, t: (0,) * len(shape),
                                       pipeline_mode=pl.Buffered(1))
    return pl.pallas_call(
        _attn_out_ffn_kernel,
        grid=(b, s // tm),
        in_specs=[
            pl.BlockSpec((1, tm, d), lambda i, t: (i, t, 0)),
            pl.BlockSpec((1, d, tm), lambda i, t: (i, 0, t)),
            const((d, d)), const((d, 1)), const((2 * D_FF, d)), const((d, D_FF)),
        ],
        out_specs=pl.BlockSpec((1, d, tm), lambda i, t: (i, 0, t)),
        out_shape=jax.ShapeDtypeStruct((b, d, s), _F32),
        compiler_params=pltpu.CompilerParams(
            dimension_semantics=("arbitrary", "arbitrary"),
            vmem_limit_bytes=V7X_VMEM_LIMIT_BYTES),
        name="attn_out_ffn",
    )(x, oT, woT, g_col, wguT, wdT)


def _conv_gate_T(hT, g_col, winT_ref):
    hg, r = _prenorm_rows(hT, g_col)
    bcu = _dot(winT_ref[...], hg)
    return bcu[:D_MODEL] * r, bcu[D_MODEL:2 * D_MODEL] * bcu[2 * D_MODEL:] * (r * r)


def _conv_z_kernel(hT_ref, g_ref, winT_ref, zT_ref):
    _, z = _conv_gate_T(hT_ref[0], g_ref[...], winT_ref)
    zT_ref[0] = z


def _conv_ffn_kernel(hT_ref, halo_ref, g1_ref, winT_ref, cw_ref, woutT_ref, g2_ref, wguT_ref,
                     wdT_ref, out_ref, zprev_ref, *, tm):
    @pl.when(pl.program_id(1) == 0)
    def _():
        zprev_ref[...] = halo_ref[...]

    h = hT_ref[0]
    gate_b, z = _conv_gate_T(h, g1_ref[...], winT_ref)
    zext = jnp.concatenate([zprev_ref[...], z], axis=1)
    z1 = pltpu.roll(zext, 1, 1)[:, V7X_LANES:]
    z2 = pltpu.roll(zext, 2, 1)[:, V7X_LANES:]
    zc = cw_ref[2] * z + cw_ref[1] * z1 + cw_ref[0] * z2
    zprev_ref[...] = z[:, tm - V7X_LANES:]
    h = h + _dot(woutT_ref[...], (gate_b * zc).astype(_BF16))
    h = h + _ffn_T(h, g2_ref[...], wguT_ref, wdT_ref)
    out_ref[0] = h.T


def _conv_z_call(hT, g_col, winT):
    b, d, s = hT.shape
    return pl.pallas_call(
        _conv_z_kernel,
        grid=(b,),
        in_specs=[
            pl.BlockSpec((1, d, s), lambda i: (i, 0, 0)),
            pl.BlockSpec((d, 1), lambda i: (0, 0)),
            pl.BlockSpec((3 * d, d), lambda i: (0, 0)),
        ],
        out_specs=pl.BlockSpec((1, d, s), lambda i: (i, 0, 0)),
        out_shape=jax.ShapeDtypeStruct((b, d, s), _F32),
        compiler_params=pltpu.CompilerParams(
            dimension_semantics=("arbitrary",),
            vmem_limit_bytes=V7X_VMEM_LIMIT_BYTES),
        name="conv_z_meta",
    )(hT, g_col, winT)


def _conv_ffn_call(hT, halo, g1_col, winT, cw, woutT, g2_col, wguT, wdT, tm):
    b, d, s = hT.shape
    const = lambda shape: pl.BlockSpec(shape, lambda i, t: (0,) * len(shape),
                                       pipeline_mode=pl.Buffered(1))
    return pl.pallas_call(
        functools.partial(_conv_ffn_kernel, tm=tm),
        grid=(b, s // tm),
        in_specs=[
            pl.BlockSpec((1, d, tm), lambda i, t: (i, 0, t)),
            const((d, V7X_LANES)), const((d, 1)), const((3 * d, d)),
            const((CONV_WIDTH, d, 1)), const((d, d)), const((d, 1)),
            const((2 * D_FF, d)), const((d, D_FF)),
        ],
        out_specs=pl.BlockSpec((1, tm, d), lambda i, t: (i, t, 0)),
        out_shape=jax.ShapeDtypeStruct((b, s, d), _F32),
        scratch_shapes=[pltpu.VMEM((d, V7X_LANES), _F32)],
        compiler_params=pltpu.CompilerParams(
            dimension_semantics=("arbitrary", "arbitrary"),
            vmem_limit_bytes=V7X_VMEM_LIMIT_BYTES),
        name="conv_ffn",
    )(hT, halo, g1_col, winT, cw, woutT, g2_col, wguT, wdT)


def _rope_tables(gain, scale, pos0, n):
    half = HEAD_DIM // 2
    inv = 1.0 / (ROPE_THETA ** (jnp.arange(0, HEAD_DIM, 2, dtype=_F32) / HEAD_DIM))
    pos = jnp.arange(pos0, pos0 + n, dtype=_F32)
    ang = pos[None, :] * inv[:, None]
    cos, sin = jnp.cos(ang), jnp.sin(ang)
    g1 = gain[:half, None].astype(_F32) * scale
    g2 = gain[half:, None].astype(_F32) * scale
    return jnp.stack([g1 * cos, g2 * sin, g2 * cos, g1 * sin])


def _token_tile(s):
    for tm in (512, 256, 128):
        if s % tm == 0:
            return tm
    raise ValueError(f"sequence length {s} must be a multiple of 128")


def kernel(x, meta_tokens, mixer_norm_g, ffn_norm_g, attn_w_qkv, attn_q_gain, attn_k_gain, attn_lambda_q1, attn_lambda_k1, attn_lambda_q2, attn_lambda_k2, attn_sub_gain, attn_w_o, conv_w_in, conv_w, conv_w_out, ffn_w_gate_up, ffn_w_down):
    b, s, d = x.shape
    assert d == D_MODEL and meta_tokens.shape == (N_META, D_MODEL)
    assert mixer_norm_g.shape[0] == 2, "layer 0 = differential attention, layer 1 = short conv"
    tm = _token_tile(s)
    tq = 256 if s % 256 == 0 else 128

    wqkvT = attn_w_qkv[0].T.astype(_BF16)
    woT = attn_w_o[0].T.astype(_BF16)
    winT = conv_w_in[0].T.astype(_BF16)
    woutT = conv_w_out[0].T.astype(_BF16)
    wguT = [ffn_w_gate_up[i].T.astype(_BF16) for i in range(2)]
    wdT = [ffn_w_down[i].T.astype(_BF16) for i in range(2)]
    cw = conv_w[0].reshape(CONV_WIDTH, D_MODEL, 1).astype(_F32)
    g_mix0_row = mixer_norm_g[0].reshape(1, D_MODEL)
    g_mix1_col = mixer_norm_g[1].reshape(D_MODEL, 1)
    g_ffn_col = [ffn_norm_g[i].reshape(D_MODEL, 1) for i in range(2)]
    gsub_col = attn_sub_gain[0].reshape(V_DIM, 1)
    lam_params = [p[0].reshape(1, HEAD_DIM) for p in
                  (attn_lambda_q1, attn_lambda_k1, attn_lambda_q2, attn_lambda_k2)]
    q_scale = HEAD_DIM ** -0.5 * math.log2(math.e)

    x_meta = jnp.zeros((1, META_PAD, D_MODEL), x.dtype).at[0, :N_META].set(meta_tokens.astype(x.dtype))
    tabq_m = _rope_tables(attn_q_gain[0], q_scale, 0, META_PAD)
    tabk_m = _rope_tables(attn_k_gain[0], 1.0, 0, META_PAD)
    qT_m, k_m, vT_m = _qkv_call(x_meta, g_mix0_row, wqkvT, tabq_m, tabk_m, META_PAD)
    oT_m = _attn_call(qT_m, k_m, vT_m, k_m[0], vT_m[0], lam_params, gsub_col,
                      tq=META_PAD, pos_off=0, has_real=False)
    h1T_m = _attn_out_ffn_call(x_meta, oT_m, woT, g_ffn_col[0], wguT[0], wdT[0], META_PAD)
    zT_m = _conv_z_call(h1T_m, g_mix1_col, winT)
    halo = jnp.zeros((D_MODEL, V7X_LANES), _F32).at[:, V7X_LANES - 2:].set(
        zT_m[0][:, N_META - 2:N_META])

    tabq = _rope_tables(attn_q_gain[0], q_scale, N_META, s)
    tabk = _rope_tables(attn_k_gain[0], 1.0, N_META, s)
    qT, k, vT = _qkv_call(x, g_mix0_row, wqkvT, tabq, tabk, tm)
    shift = (HEAD_DIM * q_scale * jnp.max(jnp.abs(attn_q_gain[0]))
             * jnp.max(jnp.abs(attn_k_gain[0]))).astype(_F32).reshape(1, 1)
    oT = lax.cond(
        shift[0, 0] <= FAST_SOFTMAX_MAX_SHIFT,
        lambda: _attn_fast_call(qT, k, vT, k_m[0], vT_m[0], lam_params, gsub_col, shift, tq=tq),
        lambda: _attn_call(qT, k, vT, k_m[0], vT_m[0], lam_params, gsub_col,
                           tq=tq, pos_off=N_META, has_real=True))
    h1T = _attn_out_ffn_call(x, oT, woT, g_ffn_col[0], wguT[0], wdT[0], tm)
    return _conv_ffn_call(h1T, halo, g_mix1_col, winT, cw, woutT, g_ffn_col[1], wguT[1], wdT[1], tm)
```

```python
import functools
import math

import jax
import jax.numpy as jnp
from jax import lax
from jax.experimental import pallas as pl
from jax.experimental.pallas import tpu as pltpu

D_MODEL = 1024
N_META = 16
N_HEADS = 8
HEAD_DIM = 64
V_DIM = 2 * HEAD_DIM
D_FF = 2816
ROPE_THETA = 10000.0
CONV_WIDTH = 3
EPS = 1e-6
LAMBDA_INIT_L0 = 0.8 - 0.6

V7X_LANES = 128
META_PAD = V7X_LANES
V7X_VMEM_LIMIT_BYTES = 56 * 1024 * 1024
N_SCORE_BUFS = 3
FAST_ATTN_HEADS_PER_STEP = 4
FAST_ATTN_IN_FLIGHT = 2
FAST_SOFTMAX_MAX_SHIFT = 50.0
MASKED = -1e30
FF_CHUNKS = ((0, 1024), (1024, 2048), (2048, D_FF))

_F32 = jnp.float32
_BF16 = jnp.bfloat16


def _dot(a, b):
    return jnp.dot(a, b, preferred_element_type=_F32)


def _prenorm_rows(hT, g_col):
    r = lax.rsqrt(jnp.mean(hT * hT, axis=0, keepdims=True) + EPS)
    return (hT * g_col).astype(_BF16), r


def _rmsnorm_rows(hT, g_col):
    ms = jnp.mean(hT * hT, axis=0, keepdims=True)
    return hT * lax.rsqrt(ms + EPS) * g_col


def _qkv_kernel(x_ref, g_ref, wT_ref, tabq_ref, tabk_ref, qT_ref, k_ref, vT_ref):
    x = x_ref[0]
    ms = jnp.mean(x * x, axis=-1, keepdims=True)
    hn = (x * lax.rsqrt(ms + EPS) * g_ref[...]).astype(_BF16)
    def proj(sec):
        return lax.dot_general(wT_ref[sec * D_MODEL:(sec + 1) * D_MODEL, :], hn,
                               (((1,), (1,)), ((), ())), preferred_element_type=_F32)

    qf, kf, vf = proj(0), proj(1), proj(2)
    tabq = tabq_ref[...]
    tabk = tabk_ref[...]
    half = HEAD_DIM // 2

    def norm_rope(y, tab):
        yn = y * lax.rsqrt(jnp.mean(y * y, axis=0, keepdims=True) + EPS)
        x1, x2 = yn[:half], yn[half:]
        return jnp.concatenate([x1 * tab[0] - x2 * tab[1], x2 * tab[2] + x1 * tab[3]], axis=0)

    for i in range(2 * N_HEADS):
        r = i * HEAD_DIM
        qT_ref[0, r:r + HEAD_DIM, :] = norm_rope(qf[r:r + HEAD_DIM], tabq).astype(_BF16)
    for h in range(N_HEADS):
        r = h * V_DIM
        kh = jnp.concatenate([norm_rope(kf[r:r + HEAD_DIM], tabk),
                              norm_rope(kf[r + HEAD_DIM:r + V_DIM], tabk)], axis=0)
        k_ref[0, :, h * V_DIM:(h + 1) * V_DIM] = kh.astype(_BF16).T
    vT_ref[0] = vf.astype(_BF16)


def _qkv_call(x, g_row, wT, tabq, tabk, tm):
    b, s, d = x.shape
    grid = (b, s // tm)
    const = lambda shape: pl.BlockSpec(shape, lambda i, t: (0,) * len(shape),
                                       pipeline_mode=pl.Buffered(1))
    return pl.pallas_call(
        _qkv_kernel,
        grid=grid,
        in_specs=[
            pl.BlockSpec((1, tm, d), lambda i, t: (i, t, 0)),
            const((1, d)),
            const((3 * d, d)),
            pl.BlockSpec((4, HEAD_DIM // 2, tm), lambda i, t: (0, 0, t)),
            pl.BlockSpec((4, HEAD_DIM // 2, tm), lambda i, t: (0, 0, t)),
        ],
        out_specs=[
            pl.BlockSpec((1, d, tm), lambda i, t: (i, 0, t)),
            pl.BlockSpec((1, tm, d), lambda i, t: (i, t, 0)),
            pl.BlockSpec((1, d, tm), lambda i, t: (i, 0, t)),
        ],
        out_shape=[
            jax.ShapeDtypeStruct((b, d, s), _BF16),
            jax.ShapeDtypeStruct((b, s, d), _BF16),
            jax.ShapeDtypeStruct((b, d, s), _BF16),
        ],
        compiler_params=pltpu.CompilerParams(
            dimension_semantics=("arbitrary", "arbitrary"),
            vmem_limit_bytes=V7X_VMEM_LIMIT_BYTES),
        name="qkv_proj",
    )(x, g_row, wT, tabq, tabk)


def _attn_kernel(qT_ref, k_ref, vT_ref, km_ref, vmT_ref, lq1_ref, lk1_ref, lq2_ref, lk2_ref,
                 gsub_ref, oT_ref, s_ref, p_ref, *, n_qblk, tq, pos_off, has_real):
    lam = (jnp.exp(jnp.sum(lq1_ref[...] * lk1_ref[...], keepdims=True))
           - jnp.exp(jnp.sum(lq2_ref[...] * lk2_ref[...], keepdims=True))
           + LAMBDA_INIT_L0)
    zero_half = jnp.zeros((HEAD_DIM, tq), _BF16)
    key_row = lax.broadcasted_iota(jnp.int32, (tq, 2 * tq), 0)
    qry_col = lax.broadcasted_iota(jnp.int32, (tq, 2 * tq), 1) % tq
    diag_mask = key_row <= qry_col
    meta_row = lax.broadcasted_iota(jnp.int32, (N_META, 2 * tq), 0)
    meta_col = lax.broadcasted_iota(jnp.int32, (N_META, 2 * tq), 1) % tq
    meta_zero_rows = jnp.zeros((META_PAD - N_META, tq), _BF16)

    def col_reduce(fn, v):
        return fn(v.reshape(v.shape[0] // 8, 8, v.shape[1]), axis=0)

    def query_block(qi, slot):
        qT = qT_ref[0, :, qi * tq:(qi + 1) * tq]
        qab = jnp.concatenate([jnp.concatenate([qT[:HEAD_DIM], zero_half], axis=0),
                               jnp.concatenate([zero_half, qT[HEAD_DIM:]], axis=0)], axis=1)
        q0 = pos_off + qi * tq
        meta_mask = None if q0 >= N_META - 1 else meta_row <= meta_col + q0
        chunks = [(0, N_META, km_ref[0:N_META, :], vmT_ref[...], meta_mask)]
        if has_real:
            for j in range(qi + 1):
                chunks.append((N_META + j * tq, tq, k_ref[0, j * tq:(j + 1) * tq, :],
                               vT_ref[0, :, j * tq:(j + 1) * tq], diag_mask if j == qi else None))
        s_buf = s_ref.at[slot]
        p_buf = p_ref.at[slot]
        m8 = jnp.full((8, 2 * tq), MASKED, _F32)
        for r0, nr, kc, _, mask in chunks:
            s = _dot(kc, qab)
            if mask is not None:
                s = jnp.where(mask, s, MASKED)
            s_buf[r0:r0 + nr, :] = s
            m8 = jnp.maximum(m8, col_reduce(jnp.max, s))
            yield False
        m = jnp.max(m8, axis=0, keepdims=True)
        yield True
        l8 = jnp.zeros((8, 2 * tq), _F32)
        for r0, nr, _, _, _ in chunks:
            p = jnp.exp2(s_buf[r0:r0 + nr, :] - m)
            l8 = l8 + col_reduce(jnp.sum, p)
            p_buf[r0:r0 + nr, :] = p.astype(_BF16)
            yield False
        l = jnp.sum(l8, axis=0, keepdims=True)
        l1, l2 = l[:, :tq], l[:, tq:]
        rho = (lam * l1 / l2).astype(_BF16)
        yield True
        acc = jnp.zeros((V_DIM, tq), _F32)
        for r0, nr, _, vc, _ in chunks:
            pb = p_buf[r0:r0 + nr, :]
            a = pb[:, :tq] - rho * pb[:, tq:]
            if nr == N_META:
                a = jnp.concatenate([a, meta_zero_rows], axis=0)
            acc = acc + _dot(vc, a)
            yield False
        on = _rmsnorm_rows(acc * (1.0 / l1), gsub_ref[...]) * (1.0 - LAMBDA_INIT_L0)
        oT_ref[0, :, qi * tq:(qi + 1) * tq] = on.astype(_BF16)
        yield True

    order = list(range(1, n_qblk, 2)) + list(range(n_qblk - 1 - (n_qblk - 1) % 2, -1, -2))
    active = []
    for stage in range(n_qblk + 2):
        if stage < n_qblk:
            active.append(query_block(order[stage], stage % N_SCORE_BUFS))
        running = list(active)
        while running:
            running = [g for g in running if not next(g)]
        if stage >= 2:
            active.pop(0)


def _attn_fast_kernel(qT_ref, k_ref, vT_ref, km_ref, vmT_ref, lq1_ref, lk1_ref, lq2_ref, lk2_ref,
                      gsub_ref, shift_ref, oT_ref, *, n_qblk, tq, heads):
    lam = (jnp.exp(jnp.sum(lq1_ref[...] * lk1_ref[...], keepdims=True))
           - jnp.exp(jnp.sum(lq2_ref[...] * lk2_ref[...], keepdims=True))
           + LAMBDA_INIT_L0)
    shift = shift_ref[...]
    zero_half = jnp.zeros((HEAD_DIM, tq), _BF16)
    key_row = lax.broadcasted_iota(jnp.int32, (tq, 2 * tq), 0)
    qry_col = lax.broadcasted_iota(jnp.int32, (tq, 2 * tq), 1) % tq
    diag_mask = key_row <= qry_col
    meta_zero_rows = jnp.zeros((META_PAD - N_META, tq), _BF16)

    def col_sum(v):
        return jnp.sum(v.reshape(v.shape[0] // 8, 8, v.shape[1]), axis=0)

    def query_block(hd, qi):
        h0 = hd * V_DIM
        qT = qT_ref[0, h0:h0 + V_DIM, qi * tq:(qi + 1) * tq]
        qab = jnp.concatenate([jnp.concatenate([qT[:HEAD_DIM], zero_half], axis=0),
                               jnp.concatenate([zero_half, qT[HEAD_DIM:]], axis=0)], axis=1)
        chunks = [(km_ref[0:N_META, h0:h0 + V_DIM], vmT_ref[h0:h0 + V_DIM, :], None)]
        for j in range(qi + 1):
            chunks.append((k_ref[0, j * tq:(j + 1) * tq, h0:h0 + V_DIM],
                           vT_ref[0, h0:h0 + V_DIM, j * tq:(j + 1) * tq],
                           diag_mask if j == qi else None))
        l8 = jnp.zeros((8, 2 * tq), _F32)
        acc1 = jnp.zeros((V_DIM, tq), _F32)
        acc2 = jnp.zeros((V_DIM, tq), _F32)
        s_next = _dot(chunks[0][0], qab)
        yield
        for ci, (kc, vc, mask) in enumerate(chunks):
            s = s_next
            if ci + 1 < len(chunks):
                s_next = _dot(chunks[ci + 1][0], qab)
                yield
            if mask is not None:
                s = jnp.where(mask, s, MASKED)
            p = jnp.exp2(s - shift)
            l8 = l8 + col_sum(p)
            pb = p.astype(_BF16)
            p1, p2 = pb[:, :tq], pb[:, tq:]
            if kc.shape[0] == N_META:
                p1 = jnp.concatenate([p1, meta_zero_rows], axis=0)
                p2 = jnp.concatenate([p2, meta_zero_rows], axis=0)
            acc1 = acc1 + _dot(vc, p1)
            acc2 = acc2 + _dot(vc, p2)
            yield
        l = jnp.sum(l8, axis=0, keepdims=True)
        o = acc1 * (1.0 / l[:, :tq]) - acc2 * (lam / l[:, tq:])
        on = _rmsnorm_rows(o, gsub_ref[...]) * (1.0 - LAMBDA_INIT_L0)
        oT_ref[0, h0:h0 + V_DIM, qi * tq:(qi + 1) * tq] = on.astype(_BF16)

    pending = [query_block(hd, qi) for hd in range(heads) for qi in range(n_qblk - 1, -1, -1)]
    active = []
    while pending or active:
        while pending and len(active) < FAST_ATTN_IN_FLIGHT:
            active.append(pending.pop(0))
        for g in list(active):
            try:
                next(g)
            except StopIteration:
                active.remove(g)


def _attn_fast_call(qT, k, vT, k_meta, vT_meta, lam_params, gsub_col, shift, *, tq):
    b, d, s = qT.shape
    hp = FAST_ATTN_HEADS_PER_STEP
    kernel = functools.partial(_attn_fast_kernel, n_qblk=s // tq, tq=tq, heads=hp)
    lam_spec = pl.BlockSpec((1, HEAD_DIM), lambda i, h: (0, 0))
    return pl.pallas_call(
        kernel,
        grid=(b, N_HEADS // hp),
        in_specs=[
            pl.BlockSpec((1, hp * V_DIM, s), lambda i, h: (i, h, 0)),
            pl.BlockSpec((1, s, hp * V_DIM), lambda i, h: (i, 0, h)),
            pl.BlockSpec((1, hp * V_DIM, s), lambda i, h: (i, h, 0)),
            pl.BlockSpec((META_PAD, hp * V_DIM), lambda i, h: (0, h)),
            pl.BlockSpec((hp * V_DIM, META_PAD), lambda i, h: (h, 0)),
            lam_spec, lam_spec, lam_spec, lam_spec,
            pl.BlockSpec((V_DIM, 1), lambda i, h: (0, 0)),
            pl.BlockSpec((1, 1), lambda i, h: (0, 0)),
        ],
        out_specs=pl.BlockSpec((1, hp * V_DIM, s), lambda i, h: (i, h, 0)),
        out_shape=jax.ShapeDtypeStruct((b, d, s), _BF16),
        compiler_params=pltpu.CompilerParams(
            dimension_semantics=("arbitrary", "arbitrary"),
            vmem_limit_bytes=V7X_VMEM_LIMIT_BYTES),
        name="diff_attn_fast",
    )(qT, k, vT, k_meta, vT_meta, *lam_params, gsub_col, shift)


def _attn_call(qT, k, vT, k_meta, vT_meta, lam_params, gsub_col, *, tq, pos_off, has_real):
    b, d, s = qT.shape
    kernel = functools.partial(_attn_kernel, n_qblk=s // tq, tq=tq, pos_off=pos_off,
                               has_real=has_real)
    lam_spec = pl.BlockSpec((1, HEAD_DIM), lambda i, h: (0, 0))
    return pl.pallas_call(
        kernel,
        grid=(b, N_HEADS),
        in_specs=[
            pl.BlockSpec((1, V_DIM, s), lambda i, h: (i, h, 0)),
            pl.BlockSpec((1, s, V_DIM), lambda i, h: (i, 0, h)),
            pl.BlockSpec((1, V_DIM, s), lambda i, h: (i, h, 0)),
            pl.BlockSpec((META_PAD, V_DIM), lambda i, h: (0, h)),
            pl.BlockSpec((V_DIM, META_PAD), lambda i, h: (h, 0)),
            lam_spec, lam_spec, lam_spec, lam_spec,
            pl.BlockSpec((V_DIM, 1), lambda i, h: (0, 0)),
        ],
        out_specs=pl.BlockSpec((1, V_DIM, s), lambda i, h: (i, h, 0)),
        out_shape=jax.ShapeDtypeStruct((b, d, s), _BF16),
        scratch_shapes=[pltpu.VMEM((N_SCORE_BUFS, N_META + (s if has_real else 0), 2 * tq), _F32),
                        pltpu.VMEM((N_SCORE_BUFS, N_META + (s if has_real else 0), 2 * tq), _BF16)],
        compiler_params=pltpu.CompilerParams(
            dimension_semantics=("arbitrary", "arbitrary"),
            vmem_limit_bytes=V7X_VMEM_LIMIT_BYTES),
        name="diff_attn",
    )(qT, k, vT, k_meta, vT_meta, *lam_params, gsub_col)


def _ffn_T(hT, g_col, wguT_ref, wdT_ref, emit=None):
    hg, r = _prenorm_rows(hT, g_col)
    acc = None
    for i, (c0, c1) in enumerate(FF_CHUNKS):
        g = _dot(wguT_ref[c0:c1, :], hg) * r
        u = _dot(wguT_ref[D_FF + c0:D_FF + c1, :], hg) * r
        a = (g * jax.nn.sigmoid(g) * u).astype(_BF16)
        if emit is not None and i + 1 == len(FF_CHUNKS):
            for r0, r1 in ((0, D_MODEL // 2), (D_MODEL // 2, D_MODEL)):
                emit(r0, r1, acc[r0:r1] + _dot(wdT_ref[r0:r1, c0:c1], a))
            return None
        part = _dot(wdT_ref[:, c0:c1], a)
        acc = part if acc is None else acc + part
    return acc


def _attn_out_ffn_kernel(x_ref, oT_ref, woT_ref, g_ref, wguT_ref, wdT_ref, hT_ref):
    h = x_ref[0].T + _dot(woT_ref[...], oT_ref[0])
    hT_ref[0] = h + _ffn_T(h, g_ref[...], wguT_ref, wdT_ref)


def _attn_out_ffn_call(x, oT, woT, g_col, wguT, wdT, tm):
    b, s, d = x.shape
    const = lambda shape: pl.BlockSpec(shape, lambda i, t: (0,) * len(shape),
                                       pipeline_mode=pl.Buffered(1))
    return pl.pallas_call(
        _attn_out_ffn_kernel,
        grid=(b, s // tm),
        in_specs=[
            pl.BlockSpec((1, tm, d), lambda i, t: (i, t, 0)),
            pl.BlockSpec((1, d, tm), lambda i, t: (i, 0, t)),
            const((d, d)), const((d, 1)), const((2 * D_FF, d)), const((d, D_FF)),
        ],
        out_specs=pl.BlockSpec((1, d, tm), lambda i, t: (i, 0, t)),
        out_shape=jax.ShapeDtypeStruct((b, d, s), _F32),
        compiler_params=pltpu.CompilerParams(
            dimension_semantics=("arbitrary", "arbitrary"),
            vmem_limit_bytes=V7X_VMEM_LIMIT_BYTES),
        name="attn_out_ffn",
    )(x, oT, woT, g_col, wguT, wdT)


def _conv_gate_T(hT, g_col, winT_ref):
    hg, r = _prenorm_rows(hT, g_col)
    bcu = _dot(winT_ref[...], hg)
    return bcu[:D_MODEL] * r, bcu[D_MODEL:2 * D_MODEL] * bcu[2 * D_MODEL:] * (r * r)


def _conv_z_kernel(hT_ref, g_ref, winT_ref, zT_ref):
    _, z = _conv_gate_T(hT_ref[0], g_ref[...], winT_ref)
    zT_ref[0] = z


def _conv_ffn_kernel(hT_ref, halo_ref, g1_ref, winT_ref, cw_ref, woutT_ref, g2_ref, wguT_ref,
                     wdT_ref, out_ref, zprev_ref, *, tm):
    @pl.when(pl.program_id(1) == 0)
    def _():
        zprev_ref[...] = halo_ref[...]

    h = hT_ref[0]
    gate_b, z = _conv_gate_T(h, g1_ref[...], winT_ref)
    zext = jnp.concatenate([zprev_ref[...], z], axis=1)
    z1 = pltpu.roll(zext, 1, 1)[:, V7X_LANES:]
    z2 = pltpu.roll(zext, 2, 1)[:, V7X_LANES:]
    zc = cw_ref[2] * z + cw_ref[1] * z1 + cw_ref[0] * z2
    zprev_ref[...] = z[:, tm - V7X_LANES:]
    h = h + _dot(woutT_ref[...], (gate_b * zc).astype(_BF16))

    def emit(r0, r1, rows):
        out_ref[0, :, r0:r1] = (h[r0:r1] + rows).T

    _ffn_T(h, g2_ref[...], wguT_ref, wdT_ref, emit)


def _conv_z_call(hT, g_col, winT):
    b, d, s = hT.shape
    return pl.pallas_call(
        _conv_z_kernel,
        grid=(b,),
        in_specs=[
            pl.BlockSpec((1, d, s), lambda i: (i, 0, 0)),
            pl.BlockSpec((d, 1), lambda i: (0, 0)),
            pl.BlockSpec((3 * d, d), lambda i: (0, 0)),
        ],
        out_specs=pl.BlockSpec((1, d, s), lambda i: (i, 0, 0)),
        out_shape=jax.ShapeDtypeStruct((b, d, s), _F32),
        compiler_params=pltpu.CompilerParams(
            dimension_semantics=("arbitrary",),
            vmem_limit_bytes=V7X_VMEM_LIMIT_BYTES),
        name="conv_z_meta",
    )(hT, g_col, winT)


def _conv_ffn_call(hT, halo, g1_col, winT, cw, woutT, g2_col, wguT, wdT, tm):
    b, d, s = hT.shape
    const = lambda shape: pl.BlockSpec(shape, lambda i, t: (0,) * len(shape),
                                       pipeline_mode=pl.Buffered(1))
    return pl.pallas_call(
        functools.partial(_conv_ffn_kernel, tm=tm),
        grid=(b, s // tm),
        in_specs=[
            pl.BlockSpec((1, d, tm), lambda i, t: (i, 0, t)),
            const((d, V7X_LANES)), const((d, 1)), const((3 * d, d)),
            const((CONV_WIDTH, d, 1)), const((d, d)), const((d, 1)),
            const((2 * D_FF, d)), const((d, D_FF)),
        ],
        out_specs=pl.BlockSpec((1, tm, d), lambda i, t: (i, t, 0)),
        out_shape=jax.ShapeDtypeStruct((b, s, d), _F32),
        scratch_shapes=[pltpu.VMEM((d, V7X_LANES), _F32)],
        compiler_params=pltpu.CompilerParams(
            dimension_semantics=("arbitrary", "arbitrary"),
            vmem_limit_bytes=V7X_VMEM_LIMIT_BYTES),
        name="conv_ffn",
    )(hT, halo, g1_col, winT, cw, woutT, g2_col, wguT, wdT)


def _rope_tables(gain, scale, pos0, n):
    half = HEAD_DIM // 2
    inv = 1.0 / (ROPE_THETA ** (jnp.arange(0, HEAD_DIM, 2, dtype=_F32) / HEAD_DIM))
    pos = jnp.arange(pos0, pos0 + n, dtype=_F32)
    ang = pos[None, :] * inv[:, None]
    cos, sin = jnp.cos(ang), jnp.sin(ang)
    g1 = gain[:half, None].astype(_F32) * scale
    g2 = gain[half:, None].astype(_F32) * scale
    return jnp.stack([g1 * cos, g2 * sin, g2 * cos, g1 * sin])


def _token_tile(s):
    for tm in (512, 256, 128):
        if s % tm == 0:
            return tm
    raise ValueError(f"sequence length {s} must be a multiple of 128")


def kernel(x, meta_tokens, mixer_norm_g, ffn_norm_g, attn_w_qkv, attn_q_gain, attn_k_gain, attn_lambda_q1, attn_lambda_k1, attn_lambda_q2, attn_lambda_k2, attn_sub_gain, attn_w_o, conv_w_in, conv_w, conv_w_out, ffn_w_gate_up, ffn_w_down):
    b, s, d = x.shape
    assert d == D_MODEL and meta_tokens.shape == (N_META, D_MODEL)
    assert mixer_norm_g.shape[0] == 2, "layer 0 = differential attention, layer 1 = short conv"
    tm = _token_tile(s)
    tq = 256 if s % 256 == 0 else 128

    wqkvT = attn_w_qkv[0].T.astype(_BF16)
    woT = attn_w_o[0].T.astype(_BF16)
    winT = conv_w_in[0].T.astype(_BF16)
    woutT = conv_w_out[0].T.astype(_BF16)
    wguT = [ffn_w_gate_up[i].T.astype(_BF16) for i in range(2)]
    wdT = [ffn_w_down[i].T.astype(_BF16) for i in range(2)]
    cw = conv_w[0].reshape(CONV_WIDTH, D_MODEL, 1).astype(_F32)
    g_mix0_row = mixer_norm_g[0].reshape(1, D_MODEL)
    g_mix1_col = mixer_norm_g[1].reshape(D_MODEL, 1)
    g_ffn_col = [ffn_norm_g[i].reshape(D_MODEL, 1) for i in range(2)]
    gsub_col = attn_sub_gain[0].reshape(V_DIM, 1)
    lam_params = [p[0].reshape(1, HEAD_DIM) for p in
                  (attn_lambda_q1, attn_lambda_k1, attn_lambda_q2, attn_lambda_k2)]
    q_scale = HEAD_DIM ** -0.5 * math.log2(math.e)

    x_meta = jnp.zeros((1, META_PAD, D_MODEL), x.dtype).at[0, :N_META].set(meta_tokens.astype(x.dtype))
    tabq_m = _rope_tables(attn_q_gain[0], q_scale, 0, META_PAD)
    tabk_m = _rope_tables(attn_k_gain[0], 1.0, 0, META_PAD)
    qT_m, k_m, vT_m = _qkv_call(x_meta, g_mix0_row, wqkvT, tabq_m, tabk_m, META_PAD)
    oT_m = _attn_call(qT_m, k_m, vT_m, k_m[0], vT_m[0], lam_params, gsub_col,
                      tq=META_PAD, pos_off=0, has_real=False)
    h1T_m = _attn_out_ffn_call(x_meta, oT_m, woT, g_ffn_col[0], wguT[0], wdT[0], META_PAD)
    zT_m = _conv_z_call(h1T_m, g_mix1_col, winT)
    halo = jnp.zeros((D_MODEL, V7X_LANES), _F32).at[:, V7X_LANES - 2:].set(
        zT_m[0][:, N_META - 2:N_META])

    tabq = _rope_tables(attn_q_gain[0], q_scale, N_META, s)
    tabk = _rope_tables(attn_k_gain[0], 1.0, N_META, s)
    qT, k, vT = _qkv_call(x, g_mix0_row, wqkvT, tabq, tabk, tm)
    shift = (HEAD_DIM * q_scale * jnp.max(jnp.abs(attn_q_gain[0]))
             * jnp.max(jnp.abs(attn_k_gain[0]))).astype(_F32).reshape(1, 1)
    oT = lax.cond(
        shift[0, 0] <= FAST_SOFTMAX_MAX_SHIFT,
        lambda: _attn_fast_call(qT, k, vT, k_m[0], vT_m[0], lam_params, gsub_col, shift, tq=tq),
        lambda: _attn_call(qT, k, vT, k_m[0], vT_m[0], lam_params, gsub_col,
                           tq=tq, pos_off=N_META, has_real=True))
    h1T = _attn_out_ffn_call(x, oT, woT, g_ffn_col[0], wguT[0], wdT[0], tm)
    return _conv_ffn_call(h1T, halo, g_mix1_col, winT, cw, woutT, g_ffn_col[1], wguT[1], wdT[1], tm)
```

```python
import functools
import math

import jax
import jax.numpy as jnp
from jax import lax
from jax.experimental import pallas as pl
from jax.experimental.pallas import tpu as pltpu

D_MODEL = 1024
N_META = 16
N_HEADS = 8
HEAD_DIM = 64
V_DIM = 2 * HEAD_DIM
D_FF = 2816
ROPE_THETA = 10000.0
CONV_WIDTH = 3
EPS = 1e-6
LAMBDA_INIT_L0 = 0.8 - 0.6

V7X_LANES = 128
V7X_SUBLANES = 8
V7X_MXU_DIM = 256
META_PAD = V7X_LANES
V7X_VMEM_LIMIT_BYTES = 56 * 1024 * 1024
N_SCORE_BUFS = 3
FAST_ATTN_HEADS_PER_STEP = 4
FAST_ATTN_IN_FLIGHT = 2
FAST_SOFTMAX_MAX_SHIFT = 50.0
MASKED = -1e30
FF_CHUNK = 4 * V7X_MXU_DIM
FF_CHUNKS = tuple((c, min(c + FF_CHUNK, D_FF)) for c in range(0, D_FF, FF_CHUNK))
TOKEN_TILES = (2 * V7X_MXU_DIM, V7X_MXU_DIM, V7X_LANES)

_F32 = jnp.float32
_BF16 = jnp.bfloat16


def _dot(a, b):
    return jnp.dot(a, b, preferred_element_type=_F32)


def _prenorm_rows(hT, g_col):
    r = lax.rsqrt(jnp.mean(hT * hT, axis=0, keepdims=True) + EPS)
    return (hT * g_col).astype(_BF16), r


def _rmsnorm_rows(hT, g_col):
    ms = jnp.mean(hT * hT, axis=0, keepdims=True)
    return hT * lax.rsqrt(ms + EPS) * g_col


def _qkv_kernel(x_ref, g_ref, wT_ref, tabq_ref, tabk_ref, qT_ref, k_ref, vT_ref):
    x = x_ref[0]
    ms = jnp.mean(x * x, axis=-1, keepdims=True)
    hn = (x * lax.rsqrt(ms + EPS) * g_ref[...]).astype(_BF16)
    def proj(sec):
        return lax.dot_general(wT_ref[sec * D_MODEL:(sec + 1) * D_MODEL, :], hn,
                               (((1,), (1,)), ((), ())), preferred_element_type=_F32)

    qf, kf, vf = proj(0), proj(1), proj(2)
    tabq = tabq_ref[...]
    tabk = tabk_ref[...]
    half = HEAD_DIM // 2

    def norm_rope(y, tab):
        yn = y * lax.rsqrt(jnp.mean(y * y, axis=0, keepdims=True) + EPS)
        x1, x2 = yn[:half], yn[half:]
        return jnp.concatenate([x1 * tab[0] - x2 * tab[1], x2 * tab[2] + x1 * tab[3]], axis=0)

    for i in range(2 * N_HEADS):
        r = i * HEAD_DIM
        qT_ref[0, r:r + HEAD_DIM, :] = norm_rope(qf[r:r + HEAD_DIM], tabq).astype(_BF16)
    for h in range(N_HEADS):
        r = h * V_DIM
        kh = jnp.concatenate([norm_rope(kf[r:r + HEAD_DIM], tabk),
                              norm_rope(kf[r + HEAD_DIM:r + V_DIM], tabk)], axis=0)
        k_ref[0, :, h * V_DIM:(h + 1) * V_DIM] = kh.astype(_BF16).T
    vT_ref[0] = vf.astype(_BF16)


def _qkv_call(x, g_row, wT, tabq, tabk, tm):
    b, s, d = x.shape
    grid = (b, s // tm)
    const = lambda shape: pl.BlockSpec(shape, lambda i, t: (0,) * len(shape),
                                       pipeline_mode=pl.Buffered(1))
    return pl.pallas_call(
        _qkv_kernel,
        grid=grid,
        in_specs=[
            pl.BlockSpec((1, tm, d), lambda i, t: (i, t, 0)),
            const((1, d)),
            const((3 * d, d)),
            pl.BlockSpec((4, HEAD_DIM // 2, tm), lambda i, t: (0, 0, t)),
            pl.BlockSpec((4, HEAD_DIM // 2, tm), lambda i, t: (0, 0, t)),
        ],
        out_specs=[
            pl.BlockSpec((1, d, tm), lambda i, t: (i, 0, t)),
            pl.BlockSpec((1, tm, d), lambda i, t: (i, t, 0)),
            pl.BlockSpec((1, d, tm), lambda i, t: (i, 0, t)),
        ],
        out_shape=[
            jax.ShapeDtypeStruct((b, d, s), _BF16),
            jax.ShapeDtypeStruct((b, s, d), _BF16),
            jax.ShapeDtypeStruct((b, d, s), _BF16),
        ],
        compiler_params=pltpu.CompilerParams(
            dimension_semantics=("arbitrary", "arbitrary"),
            vmem_limit_bytes=V7X_VMEM_LIMIT_BYTES),
        name="qkv_proj",
    )(x, g_row, wT, tabq, tabk)


def _attn_kernel(qT_ref, k_ref, vT_ref, km_ref, vmT_ref, lq1_ref, lk1_ref, lq2_ref, lk2_ref,
                 gsub_ref, oT_ref, s_ref, p_ref, *, n_qblk, tq, pos_off, has_real):
    lam = (jnp.exp(jnp.sum(lq1_ref[...] * lk1_ref[...], keepdims=True))
           - jnp.exp(jnp.sum(lq2_ref[...] * lk2_ref[...], keepdims=True))
           + LAMBDA_INIT_L0)
    zero_half = jnp.zeros((HEAD_DIM, tq), _BF16)
    key_row = lax.broadcasted_iota(jnp.int32, (tq, 2 * tq), 0)
    qry_col = lax.broadcasted_iota(jnp.int32, (tq, 2 * tq), 1) % tq
    diag_mask = key_row <= qry_col
    meta_row = lax.broadcasted_iota(jnp.int32, (N_META, 2 * tq), 0)
    meta_col = lax.broadcasted_iota(jnp.int32, (N_META, 2 * tq), 1) % tq
    meta_zero_rows = jnp.zeros((META_PAD - N_META, tq), _BF16)

    def col_reduce(fn, v):
        return fn(v.reshape(v.shape[0] // V7X_SUBLANES, V7X_SUBLANES, v.shape[1]), axis=0)

    def query_block(qi, slot):
        qT = qT_ref[0, :, qi * tq:(qi + 1) * tq]
        qab = jnp.concatenate([jnp.concatenate([qT[:HEAD_DIM], zero_half], axis=0),
                               jnp.concatenate([zero_half, qT[HEAD_DIM:]], axis=0)], axis=1)
        q0 = pos_off + qi * tq
        meta_mask = None if q0 >= N_META - 1 else meta_row <= meta_col + q0
        chunks = [(0, N_META, km_ref[0:N_META, :], vmT_ref[...], meta_mask)]
        if has_real:
            for j in range(qi + 1):
                chunks.append((N_META + j * tq, tq, k_ref[0, j * tq:(j + 1) * tq, :],
                               vT_ref[0, :, j * tq:(j + 1) * tq], diag_mask if j == qi else None))
        s_buf = s_ref.at[slot]
        p_buf = p_ref.at[slot]
        m8 = jnp.full((V7X_SUBLANES, 2 * tq), MASKED, _F32)
        for r0, nr, kc, _, mask in chunks:
            s = _dot(kc, qab)
            if mask is not None:
                s = jnp.where(mask, s, MASKED)
            s_buf[r0:r0 + nr, :] = s
            m8 = jnp.maximum(m8, col_reduce(jnp.max, s))
            yield False
        m = jnp.max(m8, axis=0, keepdims=True)
        yield True
        l8 = jnp.zeros((V7X_SUBLANES, 2 * tq), _F32)
        for r0, nr, _, _, _ in chunks:
            p = jnp.exp2(s_buf[r0:r0 + nr, :] - m)
            l8 = l8 + col_reduce(jnp.sum, p)
            p_buf[r0:r0 + nr, :] = p.astype(_BF16)
            yield False
        l = jnp.sum(l8, axis=0, keepdims=True)
        l1, l2 = l[:, :tq], l[:, tq:]
        rho = (lam * l1 / l2).astype(_BF16)
        yield True
        acc = jnp.zeros((V_DIM, tq), _F32)
        for r0, nr, _, vc, _ in chunks:
            pb = p_buf[r0:r0 + nr, :]
            a = pb[:, :tq] - rho * pb[:, tq:]
            if nr == N_META:
                a = jnp.concatenate([a, meta_zero_rows], axis=0)
            acc = acc + _dot(vc, a)
            yield False
        on = _rmsnorm_rows(acc * (1.0 / l1), gsub_ref[...]) * (1.0 - LAMBDA_INIT_L0)
        oT_ref[0, :, qi * tq:(qi + 1) * tq] = on.astype(_BF16)
        yield True

    order = list(range(1, n_qblk, 2)) + list(range(n_qblk - 1 - (n_qblk - 1) % 2, -1, -2))
    active = []
    for stage in range(n_qblk + 2):
        if stage < n_qblk:
            active.append(query_block(order[stage], stage % N_SCORE_BUFS))
        running = list(active)
        while running:
            running = [g for g in running if not next(g)]
        if stage >= 2:
            active.pop(0)


def _attn_fast_kernel(qT_ref, k_ref, vT_ref, km_ref, vmT_ref, lq1_ref, lk1_ref, lq2_ref, lk2_ref,
                      gsub_ref, shift_ref, oT_ref, *, n_qblk, tq, heads):
    lam = (jnp.exp(jnp.sum(lq1_ref[...] * lk1_ref[...], keepdims=True))
           - jnp.exp(jnp.sum(lq2_ref[...] * lk2_ref[...], keepdims=True))
           + LAMBDA_INIT_L0)
    shift = shift_ref[...]
    zero_half = jnp.zeros((HEAD_DIM, tq), _BF16)
    key_row = lax.broadcasted_iota(jnp.int32, (tq, 2 * tq), 0)
    qry_col = lax.broadcasted_iota(jnp.int32, (tq, 2 * tq), 1) % tq
    diag_mask = key_row <= qry_col
    meta_zero_rows = jnp.zeros((META_PAD - N_META, tq), _BF16)

    def col_sum(v):
        return jnp.sum(v.reshape(v.shape[0] // V7X_SUBLANES, V7X_SUBLANES, v.shape[1]), axis=0)

    def query_block(hd, qi):
        h0 = hd * V_DIM
        qT = qT_ref[0, h0:h0 + V_DIM, qi * tq:(qi + 1) * tq]
        qab = jnp.concatenate([jnp.concatenate([qT[:HEAD_DIM], zero_half], axis=0),
                               jnp.concatenate([zero_half, qT[HEAD_DIM:]], axis=0)], axis=1)
        chunks = [(km_ref[0:N_META, h0:h0 + V_DIM], vmT_ref[h0:h0 + V_DIM, :], None)]
        for j in range(qi + 1):
            chunks.append((k_ref[0, j * tq:(j + 1) * tq, h0:h0 + V_DIM],
                           vT_ref[0, h0:h0 + V_DIM, j * tq:(j + 1) * tq],
                           diag_mask if j == qi else None))
        l8 = jnp.zeros((V7X_SUBLANES, 2 * tq), _F32)
        acc1 = jnp.zeros((V_DIM, tq), _F32)
        acc2 = jnp.zeros((V_DIM, tq), _F32)
        s_next = _dot(chunks[0][0], qab)
        yield
        for ci, (kc, vc, mask) in enumerate(chunks):
            s = s_next
            if ci + 1 < len(chunks):
                s_next = _dot(chunks[ci + 1][0], qab)
                yield
            if mask is not None:
                s = jnp.where(mask, s, MASKED)
            p = jnp.exp2(s - shift)
            l8 = l8 + col_sum(p)
            pb = p.astype(_BF16)
            p1, p2 = pb[:, :tq], pb[:, tq:]
            if kc.shape[0] == N_META:
                p1 = jnp.concatenate([p1, meta_zero_rows], axis=0)
                p2 = jnp.concatenate([p2, meta_zero_rows], axis=0)
            acc1 = acc1 + _dot(vc, p1)
            acc2 = acc2 + _dot(vc, p2)
            yield
        l = jnp.sum(l8, axis=0, keepdims=True)
        o = acc1 * (1.0 / l[:, :tq]) - acc2 * (lam / l[:, tq:])
        on = _rmsnorm_rows(o, gsub_ref[...]) * (1.0 - LAMBDA_INIT_L0)
        oT_ref[0, h0:h0 + V_DIM, qi * tq:(qi + 1) * tq] = on.astype(_BF16)

    pending = [query_block(hd, qi) for hd in range(heads) for qi in range(n_qblk - 1, -1, -1)]
    active = []
    while pending or active:
        while pending and len(active) < FAST_ATTN_IN_FLIGHT:
            active.append(pending.pop(0))
        for g in list(active):
            try:
                next(g)
            except StopIteration:
                active.remove(g)


def _attn_fast_call(qT, k, vT, k_meta, vT_meta, lam_params, gsub_col, shift, *, tq):
    b, d, s = qT.shape
    hp = FAST_ATTN_HEADS_PER_STEP
    kernel = functools.partial(_attn_fast_kernel, n_qblk=s // tq, tq=tq, heads=hp)
    lam_spec = pl.BlockSpec((1, HEAD_DIM), lambda i, h: (0, 0))
    return pl.pallas_call(
        kernel,
        grid=(b, N_HEADS // hp),
        in_specs=[
            pl.BlockSpec((1, hp * V_DIM, s), lambda i, h: (i, h, 0)),
            pl.BlockSpec((1, s, hp * V_DIM), lambda i, h: (i, 0, h)),
            pl.BlockSpec((1, hp * V_DIM, s), lambda i, h: (i, h, 0)),
            pl.BlockSpec((META_PAD, hp * V_DIM), lambda i, h: (0, h)),
            pl.BlockSpec((hp * V_DIM, META_PAD), lambda i, h: (h, 0)),
            lam_spec, lam_spec, lam_spec, lam_spec,
            pl.BlockSpec((V_DIM, 1), lambda i, h: (0, 0)),
            pl.BlockSpec((1, 1), lambda i, h: (0, 0)),
        ],
        out_specs=pl.BlockSpec((1, hp * V_DIM, s), lambda i, h: (i, h, 0)),
        out_shape=jax.ShapeDtypeStruct((b, d, s), _BF16),
        compiler_params=pltpu.CompilerParams(
            dimension_semantics=("arbitrary", "arbitrary"),
            vmem_limit_bytes=V7X_VMEM_LIMIT_BYTES),
        name="diff_attn_fast",
    )(qT, k, vT, k_meta, vT_meta, *lam_params, gsub_col, shift)


def _attn_call(qT, k, vT, k_meta, vT_meta, lam_params, gsub_col, *, tq, pos_off, has_real):
    b, d, s = qT.shape
    kernel = functools.partial(_attn_kernel, n_qblk=s // tq, tq=tq, pos_off=pos_off,
                               has_real=has_real)
    lam_spec = pl.BlockSpec((1, HEAD_DIM), lambda i, h: (0, 0))
    return pl.pallas_call(
        kernel,
        grid=(b, N_HEADS),
        in_specs=[
            pl.BlockSpec((1, V_DIM, s), lambda i, h: (i, h, 0)),
            pl.BlockSpec((1, s, V_DIM), lambda i, h: (i, 0, h)),
            pl.BlockSpec((1, V_DIM, s), lambda i, h: (i, h, 0)),
            pl.BlockSpec((META_PAD, V_DIM), lambda i, h: (0, h)),
            pl.BlockSpec((V_DIM, META_PAD), lambda i, h: (h, 0)),
            lam_spec, lam_spec, lam_spec, lam_spec,
            pl.BlockSpec((V_DIM, 1), lambda i, h: (0, 0)),
        ],
        out_specs=pl.BlockSpec((1, V_DIM, s), lambda i, h: (i, h, 0)),
        out_shape=jax.ShapeDtypeStruct((b, d, s), _BF16),
        scratch_shapes=[pltpu.VMEM((N_SCORE_BUFS, N_META + (s if has_real else 0), 2 * tq), _F32),
                        pltpu.VMEM((N_SCORE_BUFS, N_META + (s if has_real else 0), 2 * tq), _BF16)],
        compiler_params=pltpu.CompilerParams(
            dimension_semantics=("arbitrary", "arbitrary"),
            vmem_limit_bytes=V7X_VMEM_LIMIT_BYTES),
        name="diff_attn",
    )(qT, k, vT, k_meta, vT_meta, *lam_params, gsub_col)


def _ffn_T(hT, g_col, wguT_ref, wdT_ref, emit=None):
    hg, r = _prenorm_rows(hT, g_col)
    acc = None
    for i, (c0, c1) in enumerate(FF_CHUNKS):
        g = _dot(wguT_ref[c0:c1, :], hg) * r
        u = _dot(wguT_ref[D_FF + c0:D_FF + c1, :], hg) * r
        a = (g * jax.nn.sigmoid(g) * u).astype(_BF16)
        if emit is not None and i + 1 == len(FF_CHUNKS):
            for r0, r1 in ((0, D_MODEL // 2), (D_MODEL // 2, D_MODEL)):
                emit(r0, r1, acc[r0:r1] + _dot(wdT_ref[r0:r1, c0:c1], a))
            return None
        part = _dot(wdT_ref[:, c0:c1], a)
        acc = part if acc is None else acc + part
    return acc


def _attn_out_ffn_kernel(x_ref, oT_ref, woT_ref, g_ref, wguT_ref, wdT_ref, hT_ref):
    h = x_ref[0].T + _dot(woT_ref[...], oT_ref[0])
    hT_ref[0] = h + _ffn_T(h, g_ref[...], wguT_ref, wdT_ref)


def _attn_out_ffn_call(x, oT, woT, g_col, wguT, wdT, tm):
    b, s, d = x.shape
    const = lambda shape: pl.BlockSpec(shape, lambda i, t: (0,) * len(shape),
                                       pipeline_mode=pl.Buffered(1))
    return pl.pallas_call(
        _attn_out_ffn_kernel,
        grid=(b, s // tm),
        in_specs=[
            pl.BlockSpec((1, tm, d), lambda i, t: (i, t, 0)),
            pl.BlockSpec((1, d, tm), lambda i, t: (i, 0, t)),
            const((d, d)), const((d, 1)), const((2 * D_FF, d)), const((d, D_FF)),
        ],
        out_specs=pl.BlockSpec((1, d, tm), lambda i, t: (i, 0, t)),
        out_shape=jax.ShapeDtypeStruct((b, d, s), _F32),
        compiler_params=pltpu.CompilerParams(
            dimension_semantics=("arbitrary", "arbitrary"),
            vmem_limit_bytes=V7X_VMEM_LIMIT_BYTES),
        name="attn_out_ffn",
    )(x, oT, woT, g_col, wguT, wdT)


def _conv_gate_T(hT, g_col, winT_ref):
    hg, r = _prenorm_rows(hT, g_col)
    bcu = _dot(winT_ref[...], hg)
    return bcu[:D_MODEL] * r, bcu[D_MODEL:2 * D_MODEL] * bcu[2 * D_MODEL:] * (r * r)


def _conv_z_kernel(hT_ref, g_ref, winT_ref, zT_ref):
    _, z = _conv_gate_T(hT_ref[0], g_ref[...], winT_ref)
    zT_ref[0] = z


def _conv_ffn_kernel(hT_ref, halo_ref, g1_ref, winT_ref, cw_ref, woutT_ref, g2_ref, wguT_ref,
                     wdT_ref, out_ref, zprev_ref, *, tm):
    @pl.when(pl.program_id(1) == 0)
    def _():
        zprev_ref[...] = halo_ref[...]

    h = hT_ref[0]
    gate_b, z = _conv_gate_T(h, g1_ref[...], winT_ref)
    zext = jnp.concatenate([zprev_ref[...], z], axis=1)
    z1 = pltpu.roll(zext, 1, 1)[:, V7X_LANES:]
    z2 = pltpu.roll(zext, 2, 1)[:, V7X_LANES:]
    zc = cw_ref[2] * z + cw_ref[1] * z1 + cw_ref[0] * z2
    zprev_ref[...] = z[:, tm - V7X_LANES:]
    h = h + _dot(woutT_ref[...], (gate_b * zc).astype(_BF16))

    def emit(r0, r1, rows):
        out_ref[0, :, r0:r1] = (h[r0:r1] + rows).T

    _ffn_T(h, g2_ref[...], wguT_ref, wdT_ref, emit)


def _conv_z_call(hT, g_col, winT):
    b, d, s = hT.shape
    return pl.pallas_call(
        _conv_z_kernel,
        grid=(b,),
        in_specs=[
            pl.BlockSpec((1, d, s), lambda i: (i, 0, 0)),
            pl.BlockSpec((d, 1), lambda i: (0, 0)),
            pl.BlockSpec((3 * d, d), lambda i: (0, 0)),
        ],
        out_specs=pl.BlockSpec((1, d, s), lambda i: (i, 0, 0)),
        out_shape=jax.ShapeDtypeStruct((b, d, s), _F32),
        compiler_params=pltpu.CompilerParams(
            dimension_semantics=("arbitrary",),
            vmem_limit_bytes=V7X_VMEM_LIMIT_BYTES),
        name="conv_z_meta",
    )(hT, g_col, winT)


def _conv_ffn_call(hT, halo, g1_col, winT, cw, woutT, g2_col, wguT, wdT, tm):
    b, d, s = hT.shape
    const = lambda shape: pl.BlockSpec(shape, lambda i, t: (0,) * len(shape),
                                       pipeline_mode=pl.Buffered(1))
    return pl.pallas_call(
        functools.partial(_conv_ffn_kernel, tm=tm),
        grid=(b, s // tm),
        in_specs=[
            pl.BlockSpec((1, d, tm), lambda i, t: (i, 0, t)),
            const((d, V7X_LANES)), const((d, 1)), const((3 * d, d)),
            const((CONV_WIDTH, d, 1)), const((d, d)), const((d, 1)),
            const((2 * D_FF, d)), const((d, D_FF)),
        ],
        out_specs=pl.BlockSpec((1, tm, d), lambda i, t: (i, t, 0)),
        out_shape=jax.ShapeDtypeStruct((b, s, d), _F32),
        scratch_shapes=[pltpu.VMEM((d, V7X_LANES), _F32)],
        compiler_params=pltpu.CompilerParams(
            dimension_semantics=("arbitrary", "arbitrary"),
            vmem_limit_bytes=V7X_VMEM_LIMIT_BYTES),
        name="conv_ffn",
    )(hT, halo, g1_col, winT, cw, woutT, g2_col, wguT, wdT)


def _rope_tables(gain, scale, pos0, n):
    half = HEAD_DIM // 2
    inv = 1.0 / (ROPE_THETA ** (jnp.arange(0, HEAD_DIM, 2, dtype=_F32) / HEAD_DIM))
    pos = jnp.arange(pos0, pos0 + n, dtype=_F32)
    ang = pos[None, :] * inv[:, None]
    cos, sin = jnp.cos(ang), jnp.sin(ang)
    g1 = gain[:half, None].astype(_F32) * scale
    g2 = gain[half:, None].astype(_F32) * scale
    return jnp.stack([g1 * cos, g2 * sin, g2 * cos, g1 * sin])


def _token_tile(s):
    for tm in TOKEN_TILES:
        if s % tm == 0:
            return tm
    raise ValueError(f"sequence length {s} must be a multiple of 128")


def kernel(x, meta_tokens, mixer_norm_g, ffn_norm_g, attn_w_qkv, attn_q_gain, attn_k_gain, attn_lambda_q1, attn_lambda_k1, attn_lambda_q2, attn_lambda_k2, attn_sub_gain, attn_w_o, conv_w_in, conv_w, conv_w_out, ffn_w_gate_up, ffn_w_down):
    b, s, d = x.shape
    assert d == D_MODEL and meta_tokens.shape == (N_META, D_MODEL)
    assert mixer_norm_g.shape[0] == 2, "layer 0 = differential attention, layer 1 = short conv"
    tm = _token_tile(s)
    tq = V7X_MXU_DIM if s % V7X_MXU_DIM == 0 else V7X_LANES

    wqkvT = attn_w_qkv[0].T.astype(_BF16)
    woT = attn_w_o[0].T.astype(_BF16)
    winT = conv_w_in[0].T.astype(_BF16)
    woutT = conv_w_out[0].T.astype(_BF16)
    wguT = [ffn_w_gate_up[i].T.astype(_BF16) for i in range(2)]
    wdT = [ffn_w_down[i].T.astype(_BF16) for i in range(2)]
    cw = conv_w[0].reshape(CONV_WIDTH, D_MODEL, 1).astype(_F32)
    g_mix0_row = mixer_norm_g[0].reshape(1, D_MODEL)
    g_mix1_col = mixer_norm_g[1].reshape(D_MODEL, 1)
    g_ffn_col = [ffn_norm_g[i].reshape(D_MODEL, 1) for i in range(2)]
    gsub_col = attn_sub_gain[0].reshape(V_DIM, 1)
    lam_params = [p[0].reshape(1, HEAD_DIM) for p in
                  (attn_lambda_q1, attn_lambda_k1, attn_lambda_q2, attn_lambda_k2)]
    q_scale = HEAD_DIM ** -0.5 * math.log2(math.e)

    x_meta = jnp.zeros((1, META_PAD, D_MODEL), x.dtype).at[0, :N_META].set(meta_tokens.astype(x.dtype))
    tabq_m = _rope_tables(attn_q_gain[0], q_scale, 0, META_PAD)
    tabk_m = _rope_tables(attn_k_gain[0], 1.0, 0, META_PAD)
    qT_m, k_m, vT_m = _qkv_call(x_meta, g_mix0_row, wqkvT, tabq_m, tabk_m, META_PAD)
    oT_m = _attn_call(qT_m, k_m, vT_m, k_m[0], vT_m[0], lam_params, gsub_col,
                      tq=META_PAD, pos_off=0, has_real=False)
    h1T_m = _attn_out_ffn_call(x_meta, oT_m, woT, g_ffn_col[0], wguT[0], wdT[0], META_PAD)
    zT_m = _conv_z_call(h1T_m, g_mix1_col, winT)
    halo = jnp.zeros((D_MODEL, V7X_LANES), _F32).at[:, V7X_LANES - 2:].set(
        zT_m[0][:, N_META - 2:N_META])

    tabq = _rope_tables(attn_q_gain[0], q_scale, N_META, s)
    tabk = _rope_tables(attn_k_gain[0], 1.0, N_META, s)
    qT, k, vT = _qkv_call(x, g_mix0_row, wqkvT, tabq, tabk, tm)
    shift = (HEAD_DIM * q_scale * jnp.max(jnp.abs(attn_q_gain[0]))
             * jnp.max(jnp.abs(attn_k_gain[0]))).astype(_F32).reshape(1, 1)
    oT = lax.cond(
        shift[0, 0] <= FAST_SOFTMAX_MAX_SHIFT,
        lambda: _attn_fast_call(qT, k, vT, k_m[0], vT_m[0], lam_params, gsub_col, shift, tq=tq),
        lambda: _attn_call(qT, k, vT, k_m[0], vT_m[0], lam_params, gsub_col,
                           tq=tq, pos_off=N_META, has_real=True))
    h1T = _attn_out_ffn_call(x, oT, woT, g_ffn_col[0], wguT[0], wdT[0], tm)
    return _conv_ffn_call(h1T, halo, g_mix1_col, winT, cw, woutT, g_ffn_col[1], wguT[1], wdT[1], tm)
```

```python
import functools
import math

import jax
import jax.numpy as jnp
from jax import lax
from jax.experimental import pallas as pl
from jax.experimental.pallas import tpu as pltpu

D_MODEL = 1024
N_META = 16
N_HEADS = 8
HEAD_DIM = 64
V_DIM = 2 * HEAD_DIM
D_FF = 2816
ROPE_THETA = 10000.0
CONV_WIDTH = 3
EPS = 1e-6
LAMBDA_INIT_L0 = 0.8 - 0.6

V7X_LANES = 128
V7X_SUBLANES = 8
V7X_MXU_DIM = 256
META_PAD = V7X_LANES
V7X_VMEM_LIMIT_BYTES = 56 * 1024 * 1024
N_SCORE_BUFS = 3
FAST_ATTN_HEADS_PER_STEP = 4
FAST_ATTN_IN_FLIGHT = 2
FAST_SOFTMAX_MAX_SHIFT = 50.0
MASKED = -1e30
FF_CHUNK = 4 * V7X_MXU_DIM
FF_CHUNKS = tuple((c, min(c + FF_CHUNK, D_FF)) for c in range(0, D_FF, FF_CHUNK))
TOKEN_TILES = (2 * V7X_MXU_DIM, V7X_MXU_DIM, V7X_LANES)

_F32 = jnp.float32
_BF16 = jnp.bfloat16


def _dot(a, b):
    return jnp.dot(a, b, preferred_element_type=_F32)


def _prenorm_rows(hT, g_col):
    r = lax.rsqrt(jnp.mean(hT * hT, axis=0, keepdims=True) + EPS)
    return (hT * g_col).astype(_BF16), r


def _rmsnorm_rows(hT, g_col):
    ms = jnp.mean(hT * hT, axis=0, keepdims=True)
    return hT * lax.rsqrt(ms + EPS) * g_col


def _staggered(gens):
    pending, active = list(gens), []
    while pending or active:
        if pending:
            active.append(pending.pop(0))
        for g in list(active):
            try:
                next(g)
            except StopIteration:
                active.remove(g)


def _qkv_kernel(x_ref, g_ref, wT_ref, tabq_ref, tabk_ref, qT_ref, k_ref, vT_ref, *, sub_tiles):
    half = HEAD_DIM // 2

    def sub_tile(c0, c1):
        x = x_ref[0, c0:c1, :]
        ms = jnp.mean(x * x, axis=-1, keepdims=True)
        hn = (x * lax.rsqrt(ms + EPS) * g_ref[...]).astype(_BF16)
        tabq = tabq_ref[:, :, c0:c1]
        tabk = tabk_ref[:, :, c0:c1]

        def proj(sec):
            return lax.dot_general(wT_ref[sec * D_MODEL:(sec + 1) * D_MODEL, :], hn,
                                   (((1,), (1,)), ((), ())), preferred_element_type=_F32)

        def norm_rope(y, tab):
            yn = y * lax.rsqrt(jnp.mean(y * y, axis=0, keepdims=True) + EPS)
            x1, x2 = yn[:half], yn[half:]
            return jnp.concatenate([x1 * tab[0] - x2 * tab[1], x2 * tab[2] + x1 * tab[3]], axis=0)

        yield
        qf = proj(0)
        yield
        kf = proj(1)
        for i in range(2 * N_HEADS):
            r = i * HEAD_DIM
            qT_ref[0, r:r + HEAD_DIM, c0:c1] = norm_rope(qf[r:r + HEAD_DIM], tabq).astype(_BF16)
        yield
        vf = proj(2)
        for h in range(N_HEADS):
            r = h * V_DIM
            kh = jnp.concatenate([norm_rope(kf[r:r + HEAD_DIM], tabk),
                                  norm_rope(kf[r + HEAD_DIM:r + V_DIM], tabk)], axis=0)
            k_ref[0, c0:c1, h * V_DIM:(h + 1) * V_DIM] = kh.astype(_BF16).T
        yield
        vT_ref[0, :, c0:c1] = vf.astype(_BF16)

    _staggered([sub_tile(c0, c1) for c0, c1 in sub_tiles])


def _qkv_call(x, g_row, wT, tabq, tabk, tm):
    b, s, d = x.shape
    grid = (b, s // tm)
    n_sub = 2 if tm % (4 * V7X_MXU_DIM) == 0 else 1
    sub = tm // n_sub
    const = lambda shape: pl.BlockSpec(shape, lambda i, t: (0,) * len(shape),
                                       pipeline_mode=pl.Buffered(1))
    return pl.pallas_call(
        functools.partial(_qkv_kernel, sub_tiles=tuple((i * sub, (i + 1) * sub) for i in range(n_sub))),
        grid=grid,
        in_specs=[
            pl.BlockSpec((1, tm, d), lambda i, t: (i, t, 0)),
            const((1, d)),
            const((3 * d, d)),
            pl.BlockSpec((4, HEAD_DIM // 2, tm), lambda i, t: (0, 0, t)),
            pl.BlockSpec((4, HEAD_DIM // 2, tm), lambda i, t: (0, 0, t)),
        ],
        out_specs=[
            pl.BlockSpec((1, d, tm), lambda i, t: (i, 0, t)),
            pl.BlockSpec((1, tm, d), lambda i, t: (i, t, 0)),
            pl.BlockSpec((1, d, tm), lambda i, t: (i, 0, t)),
        ],
        out_shape=[
            jax.ShapeDtypeStruct((b, d, s), _BF16),
            jax.ShapeDtypeStruct((b, s, d), _BF16),
            jax.ShapeDtypeStruct((b, d, s), _BF16),
        ],
        compiler_params=pltpu.CompilerParams(
            dimension_semantics=("arbitrary", "arbitrary"),
            vmem_limit_bytes=V7X_VMEM_LIMIT_BYTES),
        name="qkv_proj",
    )(x, g_row, wT, tabq, tabk)


def _attn_kernel(qT_ref, k_ref, vT_ref, km_ref, vmT_ref, lq1_ref, lk1_ref, lq2_ref, lk2_ref,
                 gsub_ref, oT_ref, s_ref, p_ref, *, n_qblk, tq, pos_off, has_real):
    lam = (jnp.exp(jnp.sum(lq1_ref[...] * lk1_ref[...], keepdims=True))
           - jnp.exp(jnp.sum(lq2_ref[...] * lk2_ref[...], keepdims=True))
           + LAMBDA_INIT_L0)
    zero_half = jnp.zeros((HEAD_DIM, tq), _BF16)
    key_row = lax.broadcasted_iota(jnp.int32, (tq, 2 * tq), 0)
    qry_col = lax.broadcasted_iota(jnp.int32, (tq, 2 * tq), 1) % tq
    diag_mask = key_row <= qry_col
    meta_row = lax.broadcasted_iota(jnp.int32, (N_META, 2 * tq), 0)
    meta_col = lax.broadcasted_iota(jnp.int32, (N_META, 2 * tq), 1) % tq
    meta_zero_rows = jnp.zeros((META_PAD - N_META, tq), _BF16)

    def col_reduce(fn, v):
        return fn(v.reshape(v.shape[0] // V7X_SUBLANES, V7X_SUBLANES, v.shape[1]), axis=0)

    def query_block(qi, slot):
        qT = qT_ref[0, :, qi * tq:(qi + 1) * tq]
        qab = jnp.concatenate([jnp.concatenate([qT[:HEAD_DIM], zero_half], axis=0),
                               jnp.concatenate([zero_half, qT[HEAD_DIM:]], axis=0)], axis=1)
        q0 = pos_off + qi * tq
        meta_mask = None if q0 >= N_META - 1 else meta_row <= meta_col + q0
        chunks = [(0, N_META, km_ref[0:N_META, :], vmT_ref[...], meta_mask)]
        if has_real:
            for j in range(qi + 1):
                chunks.append((N_META + j * tq, tq, k_ref[0, j * tq:(j + 1) * tq, :],
                               vT_ref[0, :, j * tq:(j + 1) * tq], diag_mask if j == qi else None))
        s_buf = s_ref.at[slot]
        p_buf = p_ref.at[slot]
        m8 = jnp.full((V7X_SUBLANES, 2 * tq), MASKED, _F32)
        for r0, nr, kc, _, mask in chunks:
            s = _dot(kc, qab)
            if mask is not None:
                s = jnp.where(mask, s, MASKED)
            s_buf[r0:r0 + nr, :] = s
            m8 = jnp.maximum(m8, col_reduce(jnp.max, s))
            yield False
        m = jnp.max(m8, axis=0, keepdims=True)
        yield True
        l8 = jnp.zeros((V7X_SUBLANES, 2 * tq), _F32)
        for r0, nr, _, _, _ in chunks:
            p = jnp.exp2(s_buf[r0:r0 + nr, :] - m)
            l8 = l8 + col_reduce(jnp.sum, p)
            p_buf[r0:r0 + nr, :] = p.astype(_BF16)
            yield False
        l = jnp.sum(l8, axis=0, keepdims=True)
        l1, l2 = l[:, :tq], l[:, tq:]
        rho = (lam * l1 / l2).astype(_BF16)
        yield True
        acc = jnp.zeros((V_DIM, tq), _F32)
        for r0, nr, _, vc, _ in chunks:
            pb = p_buf[r0:r0 + nr, :]
            a = pb[:, :tq] - rho * pb[:, tq:]
            if nr == N_META:
                a = jnp.concatenate([a, meta_zero_rows], axis=0)
            acc = acc + _dot(vc, a)
            yield False
        on = _rmsnorm_rows(acc * (1.0 / l1), gsub_ref[...]) * (1.0 - LAMBDA_INIT_L0)
        oT_ref[0, :, qi * tq:(qi + 1) * tq] = on.astype(_BF16)
        yield True

    order = list(range(1, n_qblk, 2)) + list(range(n_qblk - 1 - (n_qblk - 1) % 2, -1, -2))
    active = []
    for stage in range(n_qblk + 2):
        if stage < n_qblk:
            active.append(query_block(order[stage], stage % N_SCORE_BUFS))
        running = list(active)
        while running:
            running = [g for g in running if not next(g)]
        if stage >= 2:
            active.pop(0)


def _attn_fast_kernel(qT_ref, k_ref, vT_ref, km_ref, vmT_ref, lq1_ref, lk1_ref, lq2_ref, lk2_ref,
                      gsub_ref, shift_ref, oT_ref, *, n_qblk, tq, heads):
    lam = (jnp.exp(jnp.sum(lq1_ref[...] * lk1_ref[...], keepdims=True))
           - jnp.exp(jnp.sum(lq2_ref[...] * lk2_ref[...], keepdims=True))
           + LAMBDA_INIT_L0)
    shift = shift_ref[...]
    zero_half = jnp.zeros((HEAD_DIM, tq), _BF16)
    key_row = lax.broadcasted_iota(jnp.int32, (tq, 2 * tq), 0)
    qry_col = lax.broadcasted_iota(jnp.int32, (tq, 2 * tq), 1) % tq
    diag_mask = key_row <= qry_col
    meta_zero_rows = jnp.zeros((META_PAD - N_META, tq), _BF16)

    def col_sum(v):
        return jnp.sum(v.reshape(v.shape[0] // V7X_SUBLANES, V7X_SUBLANES, v.shape[1]), axis=0)

    def query_block(hd, qi):
        h0 = hd * V_DIM
        qT = qT_ref[0, h0:h0 + V_DIM, qi * tq:(qi + 1) * tq]
        qab = jnp.concatenate([jnp.concatenate([qT[:HEAD_DIM], zero_half], axis=0),
                               jnp.concatenate([zero_half, qT[HEAD_DIM:]], axis=0)], axis=1)
        chunks = [(km_ref[0:N_META, h0:h0 + V_DIM], vmT_ref[h0:h0 + V_DIM, :], None)]
        for j in range(qi + 1):
            chunks.append((k_ref[0, j * tq:(j + 1) * tq, h0:h0 + V_DIM],
                           vT_ref[0, h0:h0 + V_DIM, j * tq:(j + 1) * tq],
                           diag_mask if j == qi else None))
        l8 = jnp.zeros((V7X_SUBLANES, 2 * tq), _F32)
        acc1 = jnp.zeros((V_DIM, tq), _F32)
        acc2 = jnp.zeros((V_DIM, tq), _F32)
        s_next = _dot(chunks[0][0], qab)
        yield
        for ci, (kc, vc, mask) in enumerate(chunks):
            s = s_next
            if ci + 1 < len(chunks):
                s_next = _dot(chunks[ci + 1][0], qab)
                yield
            if mask is not None:
                s = jnp.where(mask, s, MASKED)
            p = jnp.exp2(s - shift)
            l8 = l8 + col_sum(p)
            pb = p.astype(_BF16)
            p1, p2 = pb[:, :tq], pb[:, tq:]
            if kc.shape[0] == N_META:
                p1 = jnp.concatenate([p1, meta_zero_rows], axis=0)
                p2 = jnp.concatenate([p2, meta_zero_rows], axis=0)
            acc1 = acc1 + _dot(vc, p1)
            acc2 = acc2 + _dot(vc, p2)
            yield
        l = jnp.sum(l8, axis=0, keepdims=True)
        o = acc1 * (1.0 / l[:, :tq]) - acc2 * (lam / l[:, tq:])
        on = _rmsnorm_rows(o, gsub_ref[...]) * (1.0 - LAMBDA_INIT_L0)
        oT_ref[0, h0:h0 + V_DIM, qi * tq:(qi + 1) * tq] = on.astype(_BF16)

    pending = [query_block(hd, qi) for hd in range(heads) for qi in range(n_qblk - 1, -1, -1)]
    active = []
    while pending or active:
        while pending and len(active) < FAST_ATTN_IN_FLIGHT:
            active.append(pending.pop(0))
        for g in list(active):
            try:
                next(g)
            except StopIteration:
                active.remove(g)


def _attn_fast_call(qT, k, vT, k_meta, vT_meta, lam_params, gsub_col, shift, *, tq):
    b, d, s = qT.shape
    hp = FAST_ATTN_HEADS_PER_STEP
    kernel = functools.partial(_attn_fast_kernel, n_qblk=s // tq, tq=tq, heads=hp)
    lam_spec = pl.BlockSpec((1, HEAD_DIM), lambda i, h: (0, 0))
    return pl.pallas_call(
        kernel,
        grid=(b, N_HEADS // hp),
        in_specs=[
            pl.BlockSpec((1, hp * V_DIM, s), lambda i, h: (i, h, 0)),
            pl.BlockSpec((1, s, hp * V_DIM), lambda i, h: (i, 0, h)),
            pl.BlockSpec((1, hp * V_DIM, s), lambda i, h: (i, h, 0)),
            pl.BlockSpec((META_PAD, hp * V_DIM), lambda i, h: (0, h)),
            pl.BlockSpec((hp * V_DIM, META_PAD), lambda i, h: (h, 0)),
            lam_spec, lam_spec, lam_spec, lam_spec,
            pl.BlockSpec((V_DIM, 1), lambda i, h: (0, 0)),
            pl.BlockSpec((1, 1), lambda i, h: (0, 0)),
        ],
        out_specs=pl.BlockSpec((1, hp * V_DIM, s), lambda i, h: (i, h, 0)),
        out_shape=jax.ShapeDtypeStruct((b, d, s), _BF16),
        compiler_params=pltpu.CompilerParams(
            dimension_semantics=("arbitrary", "arbitrary"),
            vmem_limit_bytes=V7X_VMEM_LIMIT_BYTES),
        name="diff_attn_fast",
    )(qT, k, vT, k_meta, vT_meta, *lam_params, gsub_col, shift)


def _attn_call(qT, k, vT, k_meta, vT_meta, lam_params, gsub_col, *, tq, pos_off, has_real):
    b, d, s = qT.shape
    kernel = functools.partial(_attn_kernel, n_qblk=s // tq, tq=tq, pos_off=pos_off,
                               has_real=has_real)
    lam_spec = pl.BlockSpec((1, HEAD_DIM), lambda i, h: (0, 0))
    return pl.pallas_call(
        kernel,
        grid=(b, N_HEADS),
        in_specs=[
            pl.BlockSpec((1, V_DIM, s), lambda i, h: (i, h, 0)),
            pl.BlockSpec((1, s, V_DIM), lambda i, h: (i, 0, h)),
            pl.BlockSpec((1, V_DIM, s), lambda i, h: (i, h, 0)),
            pl.BlockSpec((META_PAD, V_DIM), lambda i, h: (0, h)),
            pl.BlockSpec((V_DIM, META_PAD), lambda i, h: (h, 0)),
            lam_spec, lam_spec, lam_spec, lam_spec,
            pl.BlockSpec((V_DIM, 1), lambda i, h: (0, 0)),
        ],
        out_specs=pl.BlockSpec((1, V_DIM, s), lambda i, h: (i, h, 0)),
        out_shape=jax.ShapeDtypeStruct((b, d, s), _BF16),
        scratch_shapes=[pltpu.VMEM((N_SCORE_BUFS, N_META + (s if has_real else 0), 2 * tq), _F32),
                        pltpu.VMEM((N_SCORE_BUFS, N_META + (s if has_real else 0), 2 * tq), _BF16)],
        compiler_params=pltpu.CompilerParams(
            dimension_semantics=("arbitrary", "arbitrary"),
            vmem_limit_bytes=V7X_VMEM_LIMIT_BYTES),
        name="diff_attn",
    )(qT, k, vT, k_meta, vT_meta, *lam_params, gsub_col)


def _ffn_T(hT, g_col, wguT_ref, wdT_ref, emit=None):
    hg, r = _prenorm_rows(hT, g_col)
    acc = None
    for i, (c0, c1) in enumerate(FF_CHUNKS):
        g = _dot(wguT_ref[c0:c1, :], hg) * r
        u = _dot(wguT_ref[D_FF + c0:D_FF + c1, :], hg) * r
        a = (g * jax.nn.sigmoid(g) * u).astype(_BF16)
        if emit is not None and i + 1 == len(FF_CHUNKS):
            for r0, r1 in ((0, D_MODEL // 2), (D_MODEL // 2, D_MODEL)):
                emit(r0, r1, acc[r0:r1] + _dot(wdT_ref[r0:r1, c0:c1], a))
            return None
        part = _dot(wdT_ref[:, c0:c1], a)
        acc = part if acc is None else acc + part
    return acc


def _attn_out_ffn_kernel(x_ref, oT_ref, woT_ref, g_ref, wguT_ref, wdT_ref, hT_ref):
    h = x_ref[0].T + _dot(woT_ref[...], oT_ref[0])
    hT_ref[0] = h + _ffn_T(h, g_ref[...], wguT_ref, wdT_ref)


def _attn_out_ffn_call(x, oT, woT, g_col, wguT, wdT, tm):
    b, s, d = x.shape
    const = lambda shape: pl.BlockSpec(shape, lambda i, t: (0,) * len(shape),
                                       pipeline_mode=pl.Buffered(1))
    return pl.pallas_call(
        _attn_out_ffn_kernel,
        grid=(b, s // tm),
        in_specs=[
            pl.BlockSpec((1, tm, d), lambda i, t: (i, t, 0)),
            pl.BlockSpec((1, d, tm), lambda i, t: (i, 0, t)),
            const((d, d)), const((d, 1)), const((2 * D_FF, d)), const((d, D_FF)),
        ],
        out_specs=pl.BlockSpec((1, d, tm), lambda i, t: (i, 0, t)),
        out_shape=jax.ShapeDtypeStruct((b, d, s), _F32),
        compiler_params=pltpu.CompilerParams(
            dimension_semantics=("arbitrary", "arbitrary"),
            vmem_limit_bytes=V7X_VMEM_LIMIT_BYTES),
        name="attn_out_ffn",
    )(x, oT, woT, g_col, wguT, wdT)


def _conv_gate_T(hT, g_col, winT_ref):
    hg, r = _prenorm_rows(hT, g_col)
    bcu = _dot(winT_ref[...], hg)
    return bcu[:D_MODEL] * r, bcu[D_MODEL:2 * D_MODEL] * bcu[2 * D_MODEL:] * (r * r)


def _conv_z_kernel(hT_ref, g_ref, winT_ref, zT_ref):
    _, z = _conv_gate_T(hT_ref[0], g_ref[...], winT_ref)
    zT_ref[0] = z


def _conv_ffn_kernel(hT_ref, halo_ref, g1_ref, winT_ref, cw_ref, woutT_ref, g2_ref, wguT_ref,
                     wdT_ref, out_ref, zprev_ref, *, tm):
    @pl.when(pl.program_id(1) == 0)
    def _():
        zprev_ref[...] = halo_ref[...]

    h = hT_ref[0]
    gate_b, z = _conv_gate_T(h, g1_ref[...], winT_ref)
    zext = jnp.concatenate([zprev_ref[...], z], axis=1)
    z1 = pltpu.roll(zext, 1, 1)[:, V7X_LANES:]
    z2 = pltpu.roll(zext, 2, 1)[:, V7X_LANES:]
    zc = cw_ref[2] * z + cw_ref[1] * z1 + cw_ref[0] * z2
    zprev_ref[...] = z[:, tm - V7X_LANES:]
    h = h + _dot(woutT_ref[...], (gate_b * zc).astype(_BF16))

    def emit(r0, r1, rows):
        out_ref[0, :, r0:r1] = (h[r0:r1] + rows).T

    _ffn_T(h, g2_ref[...], wguT_ref, wdT_ref, emit)


def _conv_z_call(hT, g_col, winT):
    b, d, s = hT.shape
    return pl.pallas_call(
        _conv_z_kernel,
        grid=(b,),
        in_specs=[
            pl.BlockSpec((1, d, s), lambda i: (i, 0, 0)),
            pl.BlockSpec((d, 1), lambda i: (0, 0)),
            pl.BlockSpec((3 * d, d), lambda i: (0, 0)),
        ],
        out_specs=pl.BlockSpec((1, d, s), lambda i: (i, 0, 0)),
        out_shape=jax.ShapeDtypeStruct((b, d, s), _F32),
        compiler_params=pltpu.CompilerParams(
            dimension_semantics=("arbitrary",),
            vmem_limit_bytes=V7X_VMEM_LIMIT_BYTES),
        name="conv_z_meta",
    )(hT, g_col, winT)


def _conv_ffn_call(hT, halo, g1_col, winT, cw, woutT, g2_col, wguT, wdT, tm):
    b, d, s = hT.shape
    const = lambda shape: pl.BlockSpec(shape, lambda i, t: (0,) * len(shape),
                                       pipeline_mode=pl.Buffered(1))
    return pl.pallas_call(
        functools.partial(_conv_ffn_kernel, tm=tm),
        grid=(b, s // tm),
        in_specs=[
            pl.BlockSpec((1, d, tm), lambda i, t: (i, 0, t)),
            const((d, V7X_LANES)), const((d, 1)), const((3 * d, d)),
            const((CONV_WIDTH, d, 1)), const((d, d)), const((d, 1)),
            const((2 * D_FF, d)), const((d, D_FF)),
        ],
        out_specs=pl.BlockSpec((1, tm, d), lambda i, t: (i, t, 0)),
        out_shape=jax.ShapeDtypeStruct((b, s, d), _F32),
        scratch_shapes=[pltpu.VMEM((d, V7X_LANES), _F32)],
        compiler_params=pltpu.CompilerParams(
            dimension_semantics=("arbitrary", "arbitrary"),
            vmem_limit_bytes=V7X_VMEM_LIMIT_BYTES),
        name="conv_ffn",
    )(hT, halo, g1_col, winT, cw, woutT, g2_col, wguT, wdT)


def _rope_tables(gain, scale, pos0, n):
    half = HEAD_DIM // 2
    inv = 1.0 / (ROPE_THETA ** (jnp.arange(0, HEAD_DIM, 2, dtype=_F32) / HEAD_DIM))
    pos = jnp.arange(pos0, pos0 + n, dtype=_F32)
    ang = pos[None, :] * inv[:, None]
    cos, sin = jnp.cos(ang), jnp.sin(ang)
    g1 = gain[:half, None].astype(_F32) * scale
    g2 = gain[half:, None].astype(_F32) * scale
    return jnp.stack([g1 * cos, g2 * sin, g2 * cos, g1 * sin])


def _token_tile(s):
    for tm in TOKEN_TILES:
        if s % tm == 0:
            return tm
    raise ValueError(f"sequence length {s} must be a multiple of 128")


def kernel(x, meta_tokens, mixer_norm_g, ffn_norm_g, attn_w_qkv, attn_q_gain, attn_k_gain, attn_lambda_q1, attn_lambda_k1, attn_lambda_q2, attn_lambda_k2, attn_sub_gain, attn_w_o, conv_w_in, conv_w, conv_w_out, ffn_w_gate_up, ffn_w_down):
    b, s, d = x.shape
    assert d == D_MODEL and meta_tokens.shape == (N_META, D_MODEL)
    assert mixer_norm_g.shape[0] == 2, "layer 0 = differential attention, layer 1 = short conv"
    tm = _token_tile(s)
    tq = V7X_MXU_DIM if s % V7X_MXU_DIM == 0 else V7X_LANES

    wqkvT = attn_w_qkv[0].T.astype(_BF16)
    woT = attn_w_o[0].T.astype(_BF16)
    winT = conv_w_in[0].T.astype(_BF16)
    woutT = conv_w_out[0].T.astype(_BF16)
    wguT = [ffn_w_gate_up[i].T.astype(_BF16) for i in range(2)]
    wdT = [ffn_w_down[i].T.astype(_BF16) for i in range(2)]
    cw = conv_w[0].reshape(CONV_WIDTH, D_MODEL, 1).astype(_F32)
    g_mix0_row = mixer_norm_g[0].reshape(1, D_MODEL)
    g_mix1_col = mixer_norm_g[1].reshape(D_MODEL, 1)
    g_ffn_col = [ffn_norm_g[i].reshape(D_MODEL, 1) for i in range(2)]
    gsub_col = attn_sub_gain[0].reshape(V_DIM, 1)
    lam_params = [p[0].reshape(1, HEAD_DIM) for p in
                  (attn_lambda_q1, attn_lambda_k1, attn_lambda_q2, attn_lambda_k2)]
    q_scale = HEAD_DIM ** -0.5 * math.log2(math.e)

    x_meta = jnp.zeros((1, META_PAD, D_MODEL), x.dtype).at[0, :N_META].set(meta_tokens.astype(x.dtype))
    tabq_m = _rope_tables(attn_q_gain[0], q_scale, 0, META_PAD)
    tabk_m = _rope_tables(attn_k_gain[0], 1.0, 0, META_PAD)
    qT_m, k_m, vT_m = _qkv_call(x_meta, g_mix0_row, wqkvT, tabq_m, tabk_m, META_PAD)
    oT_m = _attn_call(qT_m, k_m, vT_m, k_m[0], vT_m[0], lam_params, gsub_col,
                      tq=META_PAD, pos_off=0, has_real=False)
    h1T_m = _attn_out_ffn_call(x_meta, oT_m, woT, g_ffn_col[0], wguT[0], wdT[0], META_PAD)
    zT_m = _conv_z_call(h1T_m, g_mix1_col, winT)
    halo = jnp.zeros((D_MODEL, V7X_LANES), _F32).at[:, V7X_LANES - 2:].set(
        zT_m[0][:, N_META - 2:N_META])

    tabq = _rope_tables(attn_q_gain[0], q_scale, N_META, s)
    tabk = _rope_tables(attn_k_gain[0], 1.0, N_META, s)
    qT, k, vT = _qkv_call(x, g_mix0_row, wqkvT, tabq, tabk, 2 * tm if s % (2 * tm) == 0 else tm)
    shift = (HEAD_DIM * q_scale * jnp.max(jnp.abs(attn_q_gain[0]))
             * jnp.max(jnp.abs(attn_k_gain[0]))).astype(_F32).reshape(1, 1)
    oT = lax.cond(
        shift[0, 0] <= FAST_SOFTMAX_MAX_SHIFT,
        lambda: _attn_fast_call(qT, k, vT, k_m[0], vT_m[0], lam_params, gsub_col, shift, tq=tq),
        lambda: _attn_call(qT, k, vT, k_m[0], vT_m[0], lam_params, gsub_col,
                           tq=tq, pos_off=N_META, has_real=True))
    h1T = _attn_out_ffn_call(x, oT, woT, g_ffn_col[0], wguT[0], wdT[0], tm)
    return _conv_ffn_call(h1T, halo, g_mix1_col, winT, cw, woutT, g_ffn_col[1], wguT[1], wdT[1], tm)
```

```python
import functools
import math

import jax
import jax.numpy as jnp
from jax import lax
from jax.experimental import pallas as pl
from jax.experimental.pallas import tpu as pltpu

D_MODEL = 1024
N_META = 16
N_HEADS = 8
HEAD_DIM = 64
V_DIM = 2 * HEAD_DIM
D_FF = 2816
ROPE_THETA = 10000.0
CONV_WIDTH = 3
EPS = 1e-6
LAMBDA_INIT_L0 = 0.8 - 0.6

V7X_LANES = 128
V7X_SUBLANES = 8
V7X_MXU_DIM = 256
META_PAD = V7X_LANES
V7X_VMEM_LIMIT_BYTES = 56 * 1024 * 1024
N_SCORE_BUFS = 3
FAST_ATTN_HEADS_PER_STEP = 4
FAST_ATTN_IN_FLIGHT = 2
FAST_SOFTMAX_MAX_SHIFT = 50.0
MASKED = -1e30
FF_CHUNK = 4 * V7X_MXU_DIM
FF_CHUNKS = tuple((c, min(c + FF_CHUNK, D_FF)) for c in range(0, D_FF, FF_CHUNK))
TOKEN_TILES = (2 * V7X_MXU_DIM, V7X_MXU_DIM, V7X_LANES)

_F32 = jnp.float32
_BF16 = jnp.bfloat16


def _dot(a, b):
    return jnp.dot(a, b, preferred_element_type=_F32)


def _prenorm_rows(hT, g_col):
    r = lax.rsqrt(jnp.mean(hT * hT, axis=0, keepdims=True) + EPS)
    return (hT * g_col).astype(_BF16), r


def _rmsnorm_rows(hT, g_col):
    ms = jnp.mean(hT * hT, axis=0, keepdims=True)
    return hT * lax.rsqrt(ms + EPS) * g_col


def _staggered(gens):
    pending, active = list(gens), []
    while pending or active:
        if pending:
            active.append(pending.pop(0))
        for g in list(active):
            try:
                next(g)
            except StopIteration:
                active.remove(g)


def _qkv_kernel(x_ref, g_ref, wT_ref, tabq_ref, tabk_ref, qT_ref, k_ref, vT_ref, *, sub_tiles):
    half = HEAD_DIM // 2

    def sub_tile(c0, c1):
        x = x_ref[0, c0:c1, :]
        ms = jnp.mean(x * x, axis=-1, keepdims=True)
        hn = (x * lax.rsqrt(ms + EPS) * g_ref[...]).astype(_BF16)
        tabq = tabq_ref[:, :, c0:c1]
        tabk = tabk_ref[:, :, c0:c1]

        def proj(sec):
            return lax.dot_general(wT_ref[sec * D_MODEL:(sec + 1) * D_MODEL, :], hn,
                                   (((1,), (1,)), ((), ())), preferred_element_type=_F32)

        def norm_rope(y, tab):
            yn = y * lax.rsqrt(jnp.mean(y * y, axis=0, keepdims=True) + EPS)
            x1, x2 = yn[:half], yn[half:]
            return jnp.concatenate([x1 * tab[0] - x2 * tab[1], x2 * tab[2] + x1 * tab[3]], axis=0)

        yield
        qf = proj(0)
        yield
        kf = proj(1)
        for i in range(2 * N_HEADS):
            r = i * HEAD_DIM
            qT_ref[0, r:r + HEAD_DIM, c0:c1] = norm_rope(qf[r:r + HEAD_DIM], tabq).astype(_BF16)
        yield
        vf = proj(2)
        for h in range(N_HEADS):
            r = h * V_DIM
            kh = jnp.concatenate([norm_rope(kf[r:r + HEAD_DIM], tabk),
                                  norm_rope(kf[r + HEAD_DIM:r + V_DIM], tabk)], axis=0)
            k_ref[0, c0:c1, h * V_DIM:(h + 1) * V_DIM] = kh.astype(_BF16).T
        yield
        vT_ref[0, :, c0:c1] = vf.astype(_BF16)

    _staggered([sub_tile(c0, c1) for c0, c1 in sub_tiles])


def _qkv_call(x, g_row, wT, tabq, tabk, tm):
    b, s, d = x.shape
    grid = (b, s // tm)
    n_sub = 2 if tm % (4 * V7X_MXU_DIM) == 0 else 1
    sub = tm // n_sub
    const = lambda shape: pl.BlockSpec(shape, lambda i, t: (0,) * len(shape),
                                       pipeline_mode=pl.Buffered(1))
    return pl.pallas_call(
        functools.partial(_qkv_kernel, sub_tiles=tuple((i * sub, (i + 1) * sub) for i in range(n_sub))),
        grid=grid,
        in_specs=[
            pl.BlockSpec((1, tm, d), lambda i, t: (i, t, 0)),
            const((1, d)),
            const((3 * d, d)),
            pl.BlockSpec((4, HEAD_DIM // 2, tm), lambda i, t: (0, 0, t)),
            pl.BlockSpec((4, HEAD_DIM // 2, tm), lambda i, t: (0, 0, t)),
        ],
        out_specs=[
            pl.BlockSpec((1, d, tm), lambda i, t: (i, 0, t)),
            pl.BlockSpec((1, tm, d), lambda i, t: (i, t, 0)),
            pl.BlockSpec((1, d, tm), lambda i, t: (i, 0, t)),
        ],
        out_shape=[
            jax.ShapeDtypeStruct((b, d, s), _BF16),
            jax.ShapeDtypeStruct((b, s, d), _BF16),
            jax.ShapeDtypeStruct((b, d, s), _BF16),
        ],
        compiler_params=pltpu.CompilerParams(
            dimension_semantics=("arbitrary", "arbitrary"),
            vmem_limit_bytes=V7X_VMEM_LIMIT_BYTES),
        name="qkv_proj",
    )(x, g_row, wT, tabq, tabk)


def _attn_kernel(qT_ref, k_ref, vT_ref, km_ref, vmT_ref, lq1_ref, lk1_ref, lq2_ref, lk2_ref,
                 gsub_ref, oT_ref, s_ref, p_ref, *, n_qblk, tq, pos_off, has_real):
    lam = (jnp.exp(jnp.sum(lq1_ref[...] * lk1_ref[...], keepdims=True))
           - jnp.exp(jnp.sum(lq2_ref[...] * lk2_ref[...], keepdims=True))
           + LAMBDA_INIT_L0)
    zero_half = jnp.zeros((HEAD_DIM, tq), _BF16)
    key_row = lax.broadcasted_iota(jnp.int32, (tq, 2 * tq), 0)
    qry_col = lax.broadcasted_iota(jnp.int32, (tq, 2 * tq), 1) % tq
    diag_mask = key_row <= qry_col
    meta_row = lax.broadcasted_iota(jnp.int32, (N_META, 2 * tq), 0)
    meta_col = lax.broadcasted_iota(jnp.int32, (N_META, 2 * tq), 1) % tq
    meta_zero_rows = jnp.zeros((META_PAD - N_META, tq), _BF16)

    def col_reduce(fn, v):
        return fn(v.reshape(v.shape[0] // V7X_SUBLANES, V7X_SUBLANES, v.shape[1]), axis=0)

    def query_block(qi, slot):
        qT = qT_ref[0, :, qi * tq:(qi + 1) * tq]
        qab = jnp.concatenate([jnp.concatenate([qT[:HEAD_DIM], zero_half], axis=0),
                               jnp.concatenate([zero_half, qT[HEAD_DIM:]], axis=0)], axis=1)
        q0 = pos_off + qi * tq
        meta_mask = None if q0 >= N_META - 1 else meta_row <= meta_col + q0
        chunks = [(0, N_META, km_ref[0:N_META, :], vmT_ref[...], meta_mask)]
        if has_real:
            for j in range(qi + 1):
                chunks.append((N_META + j * tq, tq, k_ref[0, j * tq:(j + 1) * tq, :],
                               vT_ref[0, :, j * tq:(j + 1) * tq], diag_mask if j == qi else None))
        s_buf = s_ref.at[slot]
        p_buf = p_ref.at[slot]
        m8 = jnp.full((V7X_SUBLANES, 2 * tq), MASKED, _F32)
        for r0, nr, kc, _, mask in chunks:
            s = _dot(kc, qab)
            if mask is not None:
                s = jnp.where(mask, s, MASKED)
            s_buf[r0:r0 + nr, :] = s
            m8 = jnp.maximum(m8, col_reduce(jnp.max, s))
            yield False
        m = jnp.max(m8, axis=0, keepdims=True)
        yield True
        l8 = jnp.zeros((V7X_SUBLANES, 2 * tq), _F32)
        for r0, nr, _, _, _ in chunks:
            p = jnp.exp2(s_buf[r0:r0 + nr, :] - m)
            l8 = l8 + col_reduce(jnp.sum, p)
            p_buf[r0:r0 + nr, :] = p.astype(_BF16)
            yield False
        l = jnp.sum(l8, axis=0, keepdims=True)
        l1, l2 = l[:, :tq], l[:, tq:]
        rho = (lam * l1 / l2).astype(_BF16)
        yield True
        acc = jnp.zeros((V_DIM, tq), _F32)
        for r0, nr, _, vc, _ in chunks:
            pb = p_buf[r0:r0 + nr, :]
            a = pb[:, :tq] - rho * pb[:, tq:]
            if nr == N_META:
                a = jnp.concatenate([a, meta_zero_rows], axis=0)
            acc = acc + _dot(vc, a)
            yield False
        on = _rmsnorm_rows(acc * (1.0 / l1), gsub_ref[...]) * (1.0 - LAMBDA_INIT_L0)
        oT_ref[0, :, qi * tq:(qi + 1) * tq] = on.astype(_BF16)
        yield True

    order = list(range(1, n_qblk, 2)) + list(range(n_qblk - 1 - (n_qblk - 1) % 2, -1, -2))
    active = []
    for stage in range(n_qblk + 2):
        if stage < n_qblk:
            active.append(query_block(order[stage], stage % N_SCORE_BUFS))
        running = list(active)
        while running:
            running = [g for g in running if not next(g)]
        if stage >= 2:
            active.pop(0)


def _attn_fast_kernel(qT_ref, k_ref, vT_ref, km_ref, vmT_ref, lq1_ref, lk1_ref, lq2_ref, lk2_ref,
                      gsub_ref, shift_ref, oT_ref, *, n_qblk, tq, heads):
    lam = (jnp.exp(jnp.sum(lq1_ref[...] * lk1_ref[...], keepdims=True))
           - jnp.exp(jnp.sum(lq2_ref[...] * lk2_ref[...], keepdims=True))
           + LAMBDA_INIT_L0)
    shift = shift_ref[...]
    zero_half = jnp.zeros((HEAD_DIM, tq), _BF16)
    key_row = lax.broadcasted_iota(jnp.int32, (tq, 2 * tq), 0)
    qry_col = lax.broadcasted_iota(jnp.int32, (tq, 2 * tq), 1) % tq
    diag_mask = key_row <= qry_col
    meta_zero_rows = jnp.zeros((META_PAD - N_META, tq), _BF16)

    def col_sum(v):
        return jnp.sum(v.reshape(v.shape[0] // V7X_SUBLANES, V7X_SUBLANES, v.shape[1]), axis=0)

    def query_block(hd, qi):
        h0 = hd * V_DIM
        qT = qT_ref[0, h0:h0 + V_DIM, qi * tq:(qi + 1) * tq]
        qab = jnp.concatenate([jnp.concatenate([qT[:HEAD_DIM], zero_half], axis=0),
                               jnp.concatenate([zero_half, qT[HEAD_DIM:]], axis=0)], axis=1)
        chunks = [(km_ref[0:N_META, h0:h0 + V_DIM], vmT_ref[h0:h0 + V_DIM, :], None)]
        for j in range(qi + 1):
            chunks.append((k_ref[0, j * tq:(j + 1) * tq, h0:h0 + V_DIM],
                           vT_ref[0, h0:h0 + V_DIM, j * tq:(j + 1) * tq],
                           diag_mask if j == qi else None))
        l8 = jnp.zeros((V7X_SUBLANES, 2 * tq), _F32)
        acc1 = jnp.zeros((V_DIM, tq), _F32)
        acc2 = jnp.zeros((V_DIM, tq), _F32)
        s_next = _dot(chunks[0][0], qab)
        yield
        for ci, (kc, vc, mask) in enumerate(chunks):
            s = s_next
            if ci + 1 < len(chunks):
                s_next = _dot(chunks[ci + 1][0], qab)
                yield
            if mask is not None:
                s = jnp.where(mask, s, MASKED)
            p = jnp.exp2(s - shift)
            l8 = l8 + col_sum(p)
            pb = p.astype(_BF16)
            p1, p2 = pb[:, :tq], pb[:, tq:]
            if kc.shape[0] == N_META:
                p1 = jnp.concatenate([p1, meta_zero_rows], axis=0)
                p2 = jnp.concatenate([p2, meta_zero_rows], axis=0)
            acc1 = acc1 + _dot(vc, p1)
            acc2 = acc2 + _dot(vc, p2)
            yield
        l = jnp.sum(l8, axis=0, keepdims=True)
        o = acc1 * (1.0 / l[:, :tq]) - acc2 * (lam / l[:, tq:])
        on = _rmsnorm_rows(o, gsub_ref[...]) * (1.0 - LAMBDA_INIT_L0)
        oT_ref[0, h0:h0 + V_DIM, qi * tq:(qi + 1) * tq] = on.astype(_BF16)

    pending = [query_block(hd, qi) for hd in range(heads) for qi in range(n_qblk - 1, -1, -1)]
    active = []
    while pending or active:
        while pending and len(active) < FAST_ATTN_IN_FLIGHT:
            active.append(pending.pop(0))
        for g in list(active):
            try:
                next(g)
            except StopIteration:
                active.remove(g)


def _attn_fast_call(qT, k, vT, k_meta, vT_meta, lam_params, gsub_col, shift, *, tq):
    b, d, s = qT.shape
    hp = FAST_ATTN_HEADS_PER_STEP
    kernel = functools.partial(_attn_fast_kernel, n_qblk=s // tq, tq=tq, heads=hp)
    lam_spec = pl.BlockSpec((1, HEAD_DIM), lambda i, h: (0, 0))
    return pl.pallas_call(
        kernel,
        grid=(b, N_HEADS // hp),
        in_specs=[
            pl.BlockSpec((1, hp * V_DIM, s), lambda i, h: (i, h, 0)),
            pl.BlockSpec((1, s, hp * V_DIM), lambda i, h: (i, 0, h)),
            pl.BlockSpec((1, hp * V_DIM, s), lambda i, h: (i, h, 0)),
            pl.BlockSpec((META_PAD, hp * V_DIM), lambda i, h: (0, h)),
            pl.BlockSpec((hp * V_DIM, META_PAD), lambda i, h: (h, 0)),
            lam_spec, lam_spec, lam_spec, lam_spec,
            pl.BlockSpec((V_DIM, 1), lambda i, h: (0, 0)),
            pl.BlockSpec((1, 1), lambda i, h: (0, 0)),
        ],
        out_specs=pl.BlockSpec((1, hp * V_DIM, s), lambda i, h: (i, h, 0)),
        out_shape=jax.ShapeDtypeStruct((b, d, s), _BF16),
        compiler_params=pltpu.CompilerParams(
            dimension_semantics=("arbitrary", "arbitrary"),
            vmem_limit_bytes=V7X_VMEM_LIMIT_BYTES),
        name="diff_attn_fast",
    )(qT, k, vT, k_meta, vT_meta, *lam_params, gsub_col, shift)


def _attn_call(qT, k, vT, k_meta, vT_meta, lam_params, gsub_col, *, tq, pos_off, has_real):
    b, d, s = qT.shape
    kernel = functools.partial(_attn_kernel, n_qblk=s // tq, tq=tq, pos_off=pos_off,
                               has_real=has_real)
    lam_spec = pl.BlockSpec((1, HEAD_DIM), lambda i, h: (0, 0))
    return pl.pallas_call(
        kernel,
        grid=(b, N_HEADS),
        in_specs=[
            pl.BlockSpec((1, V_DIM, s), lambda i, h: (i, h, 0)),
            pl.BlockSpec((1, s, V_DIM), lambda i, h: (i, 0, h)),
            pl.BlockSpec((1, V_DIM, s), lambda i, h: (i, h, 0)),
            pl.BlockSpec((META_PAD, V_DIM), lambda i, h: (0, h)),
            pl.BlockSpec((V_DIM, META_PAD), lambda i, h: (h, 0)),
            lam_spec, lam_spec, lam_spec, lam_spec,
            pl.BlockSpec((V_DIM, 1), lambda i, h: (0, 0)),
        ],
        out_specs=pl.BlockSpec((1, V_DIM, s), lambda i, h: (i, h, 0)),
        out_shape=jax.ShapeDtypeStruct((b, d, s), _BF16),
        scratch_shapes=[pltpu.VMEM((N_SCORE_BUFS, N_META + (s if has_real else 0), 2 * tq), _F32),
                        pltpu.VMEM((N_SCORE_BUFS, N_META + (s if has_real else 0), 2 * tq), _BF16)],
        compiler_params=pltpu.CompilerParams(
            dimension_semantics=("arbitrary", "arbitrary"),
            vmem_limit_bytes=V7X_VMEM_LIMIT_BYTES),
        name="diff_attn",
    )(qT, k, vT, k_meta, vT_meta, *lam_params, gsub_col)


def _ffn_T(hT, g_col, wguT_ref, wdT_ref, emit=None):
    hg, r = _prenorm_rows(hT, g_col)
    acc = None
    for i, (c0, c1) in enumerate(FF_CHUNKS):
        g = _dot(wguT_ref[c0:c1, :], hg) * r
        u = _dot(wguT_ref[D_FF + c0:D_FF + c1, :], hg) * r
        a = (g * jax.nn.sigmoid(g) * u).astype(_BF16)
        if emit is not None and i + 1 == len(FF_CHUNKS):
            for r0, r1 in ((0, D_MODEL // 2), (D_MODEL // 2, D_MODEL)):
                emit(r0, r1, acc[r0:r1] + _dot(wdT_ref[r0:r1, c0:c1], a))
            return None
        part = _dot(wdT_ref[:, c0:c1], a)
        acc = part if acc is None else acc + part
    return acc


def _attn_out_ffn_kernel(x_ref, oT_ref, woT_ref, g_ref, wguT_ref, wdT_ref, hT_ref):
    h = x_ref[0].T + _dot(woT_ref[...], oT_ref[0])
    hT_ref[0] = h + _ffn_T(h, g_ref[...], wguT_ref, wdT_ref)


def _attn_out_ffn_call(x, oT, woT, g_col, wguT, wdT, tm):
    b, s, d = x.shape
    const = lambda shape: pl.BlockSpec(shape, lambda i, t: (0,) * len(shape),
                                       pipeline_mode=pl.Buffered(1))
    return pl.pallas_call(
        _attn_out_ffn_kernel,
        grid=(b, s // tm),
        in_specs=[
            pl.BlockSpec((1, tm, d), lambda i, t: (i, t, 0)),
            pl.BlockSpec((1, d, tm), lambda i, t: (i, 0, t)),
            const((d, d)), const((d, 1)), const((2 * D_FF, d)), const((d, D_FF)),
        ],
        out_specs=pl.BlockSpec((1, d, tm), lambda i, t: (i, 0, t)),
        out_shape=jax.ShapeDtypeStruct((b, d, s), _F32),
        compiler_params=pltpu.CompilerParams(
            dimension_semantics=("arbitrary", "arbitrary"),
            vmem_limit_bytes=V7X_VMEM_LIMIT_BYTES),
        name="attn_out_ffn",
    )(x, oT, woT, g_col, wguT, wdT)


def _conv_z_T(hg, r, winT_ref):
    cu = _dot(winT_ref[D_MODEL:, :], hg)
    return cu[:D_MODEL] * cu[D_MODEL:] * (r * r)


def _conv_z_kernel(hT_ref, g_ref, winT_ref, zT_ref):
    zT_ref[0] = _conv_z_T(*_prenorm_rows(hT_ref[0], g_ref[...]), winT_ref)


def _conv_ffn_kernel(hT_ref, halo_ref, g1_ref, winT_ref, cw_ref, woutT_ref, g2_ref, wguT_ref,
                     wdT_ref, out_ref, zprev_ref, *, tm):
    @pl.when(pl.program_id(1) == 0)
    def _():
        zprev_ref[...] = halo_ref[...]

    h = hT_ref[0]
    hg, r = _prenorm_rows(h, g1_ref[...])
    z = _conv_z_T(hg, r, winT_ref)
    zext = jnp.concatenate([zprev_ref[...], z], axis=1)
    z1 = pltpu.roll(zext, 1, 1)[:, V7X_LANES:]
    z2 = pltpu.roll(zext, 2, 1)[:, V7X_LANES:]
    zc = cw_ref[2] * z + cw_ref[1] * z1 + cw_ref[0] * z2
    zprev_ref[...] = z[:, tm - V7X_LANES:]
    gate_b = _dot(winT_ref[:D_MODEL, :], hg) * r
    h = h + _dot(woutT_ref[...], (gate_b * zc).astype(_BF16))

    def emit(r0, r1, rows):
        out_ref[0, :, r0:r1] = (h[r0:r1] + rows).T

    _ffn_T(h, g2_ref[...], wguT_ref, wdT_ref, emit)


def _conv_z_call(hT, g_col, winT):
    b, d, s = hT.shape
    return pl.pallas_call(
        _conv_z_kernel,
        grid=(b,),
        in_specs=[
            pl.BlockSpec((1, d, s), lambda i: (i, 0, 0)),
            pl.BlockSpec((d, 1), lambda i: (0, 0)),
            pl.BlockSpec((3 * d, d), lambda i: (0, 0)),
        ],
        out_specs=pl.BlockSpec((1, d, s), lambda i: (i, 0, 0)),
        out_shape=jax.ShapeDtypeStruct((b, d, s), _F32),
        compiler_params=pltpu.CompilerParams(
            dimension_semantics=("arbitrary",),
            vmem_limit_bytes=V7X_VMEM_LIMIT_BYTES),
        name="conv_z_meta",
    )(hT, g_col, winT)


def _conv_ffn_call(hT, halo, g1_col, winT, cw, woutT, g2_col, wguT, wdT, tm):
    b, d, s = hT.shape
    const = lambda shape: pl.BlockSpec(shape, lambda i, t: (0,) * len(shape),
                                       pipeline_mode=pl.Buffered(1))
    return pl.pallas_call(
        functools.partial(_conv_ffn_kernel, tm=tm),
        grid=(b, s // tm),
        in_specs=[
            pl.BlockSpec((1, d, tm), lambda i, t: (i, 0, t)),
            const((d, V7X_LANES)), const((d, 1)), const((3 * d, d)),
            const((CONV_WIDTH, d, 1)), const((d, d)), const((d, 1)),
            const((2 * D_FF, d)), const((d, D_FF)),
        ],
        out_specs=pl.BlockSpec((1, tm, d), lambda i, t: (i, t, 0)),
        out_shape=jax.ShapeDtypeStruct((b, s, d), _F32),
        scratch_shapes=[pltpu.VMEM((d, V7X_LANES), _F32)],
        compiler_params=pltpu.CompilerParams(
            dimension_semantics=("arbitrary", "arbitrary"),
            vmem_limit_bytes=V7X_VMEM_LIMIT_BYTES),
        name="conv_ffn",
    )(hT, halo, g1_col, winT, cw, woutT, g2_col, wguT, wdT)


def _rope_tables(gain, scale, pos0, n):
    half = HEAD_DIM // 2
    inv = 1.0 / (ROPE_THETA ** (jnp.arange(0, HEAD_DIM, 2, dtype=_F32) / HEAD_DIM))
    pos = jnp.arange(pos0, pos0 + n, dtype=_F32)
    ang = pos[None, :] * inv[:, None]
    cos, sin = jnp.cos(ang), jnp.sin(ang)
    g1 = gain[:half, None].astype(_F32) * scale
    g2 = gain[half:, None].astype(_F32) * scale
    return jnp.stack([g1 * cos, g2 * sin, g2 * cos, g1 * sin])


def _token_tile(s):
    for tm in TOKEN_TILES:
        if s % tm == 0:
            return tm
    raise ValueError(f"sequence length {s} must be a multiple of 128")


def kernel(x, meta_tokens, mixer_norm_g, ffn_norm_g, attn_w_qkv, attn_q_gain, attn_k_gain, attn_lambda_q1, attn_lambda_k1, attn_lambda_q2, attn_lambda_k2, attn_sub_gain, attn_w_o, conv_w_in, conv_w, conv_w_out, ffn_w_gate_up, ffn_w_down):
    b, s, d = x.shape
    assert d == D_MODEL and meta_tokens.shape == (N_META, D_MODEL)
    assert mixer_norm_g.shape[0] == 2, "layer 0 = differential attention, layer 1 = short conv"
    tm = _token_tile(s)
    tq = V7X_MXU_DIM if s % V7X_MXU_DIM == 0 else V7X_LANES

    wqkvT = attn_w_qkv[0].T.astype(_BF16)
    woT = attn_w_o[0].T.astype(_BF16)
    winT = conv_w_in[0].T.astype(_BF16)
    woutT = conv_w_out[0].T.astype(_BF16)
    wguT = [ffn_w_gate_up[i].T.astype(_BF16) for i in range(2)]
    wdT = [ffn_w_down[i].T.astype(_BF16) for i in range(2)]
    cw = conv_w[0].reshape(CONV_WIDTH, D_MODEL, 1).astype(_F32)
    g_mix0_row = mixer_norm_g[0].reshape(1, D_MODEL)
    g_mix1_col = mixer_norm_g[1].reshape(D_MODEL, 1)
    g_ffn_col = [ffn_norm_g[i].reshape(D_MODEL, 1) for i in range(2)]
    gsub_col = attn_sub_gain[0].reshape(V_DIM, 1)
    lam_params = [p[0].reshape(1, HEAD_DIM) for p in
                  (attn_lambda_q1, attn_lambda_k1, attn_lambda_q2, attn_lambda_k2)]
    q_scale = HEAD_DIM ** -0.5 * math.log2(math.e)

    x_meta = jnp.zeros((1, META_PAD, D_MODEL), x.dtype).at[0, :N_META].set(meta_tokens.astype(x.dtype))
    tabq_m = _rope_tables(attn_q_gain[0], q_scale, 0, META_PAD)
    tabk_m = _rope_tables(attn_k_gain[0], 1.0, 0, META_PAD)
    qT_m, k_m, vT_m = _qkv_call(x_meta, g_mix0_row, wqkvT, tabq_m, tabk_m, META_PAD)
    oT_m = _attn_call(qT_m, k_m, vT_m, k_m[0], vT_m[0], lam_params, gsub_col,
                      tq=META_PAD, pos_off=0, has_real=False)
    h1T_m = _attn_out_ffn_call(x_meta, oT_m, woT, g_ffn_col[0], wguT[0], wdT[0], META_PAD)
    zT_m = _conv_z_call(h1T_m, g_mix1_col, winT)
    halo = jnp.zeros((D_MODEL, V7X_LANES), _F32).at[:, V7X_LANES - 2:].set(
        zT_m[0][:, N_META - 2:N_META])

    tabq = _rope_tables(attn_q_gain[0], q_scale, N_META, s)
    tabk = _rope_tables(attn_k_gain[0], 1.0, N_META, s)
    qT, k, vT = _qkv_call(x, g_mix0_row, wqkvT, tabq, tabk, 2 * tm if s % (2 * tm) == 0 else tm)
    shift = (HEAD_DIM * q_scale * jnp.max(jnp.abs(attn_q_gain[0]))
             * jnp.max(jnp.abs(attn_k_gain[0]))).astype(_F32).reshape(1, 1)
    oT = lax.cond(
        shift[0, 0] <= FAST_SOFTMAX_MAX_SHIFT,
        lambda: _attn_fast_call(qT, k, vT, k_m[0], vT_m[0], lam_params, gsub_col, shift, tq=tq),
        lambda: _attn_call(qT, k, vT, k_m[0], vT_m[0], lam_params, gsub_col,
                           tq=tq, pos_off=N_META, has_real=True))
    h1T = _attn_out_ffn_call(x, oT, woT, g_ffn_col[0], wguT[0], wdT[0], tm)
    return _conv_ffn_call(h1T, halo, g_mix1_col, winT, cw, woutT, g_ffn_col[1], wguT[1], wdT[1], tm)
```
